```python
import math
import jax, jax.numpy as jnp
from jax import lax
import numpy as np

D_MODEL = 1024
BATCH = 2
SEQ = 8192
DEPTH = 4

CHUNK = 64
N_MIXERS = 3
EPS = 1e-6

SSD_EXPAND = 2
SSD_D_INNER = SSD_EXPAND * D_MODEL
SSD_HEAD_DIM = 64
SSD_HEADS = SSD_D_INNER // SSD_HEAD_DIM
SSD_GROUPS = 4
SSD_STATE = 128
SSD_CONV = 4
SSD_CONV_DIM = SSD_D_INNER + 2 * SSD_GROUPS * SSD_STATE
SSD_IN_DIM = SSD_D_INNER + SSD_CONV_DIM + SSD_HEADS

MLA_HEADS = 8
MLA_NOPE = 128
MLA_ROPE = 64
MLA_V = 128
MLA_QK = MLA_NOPE + MLA_ROPE
MLA_Q_RANK = 256
MLA_KV_RANK = 256
MLA_IN_DIM = MLA_Q_RANK + MLA_KV_RANK + MLA_ROPE
Q_BLOCK = 128
ROPE_BASE = 10000.0

RET_HEADS = D_MODEL // 256
RET_QK = 256
RET_V = 512
RET_IN_DIM = 2 * RET_HEADS * RET_QK + 2 * RET_HEADS * RET_V

FFN_DIM = 2816
N_EXPERTS = 8
TOP_K = 2
EXPERT_DIM = 3584
MOE_BLOCK = 256

N_SSD = len(range(0, DEPTH, N_MIXERS))
N_MLA = len(range(1, DEPTH, N_MIXERS))
N_RET = len(range(2, DEPTH, N_MIXERS))
N_DENSE = (DEPTH + 1) // 2
N_MOE = DEPTH // 2

kernel_name = "hybrid_ssd_mla_retention_moe_adaln"


def rms_norm(x, g):
    xf = x.astype(jnp.float32)
    y = xf * lax.rsqrt(jnp.mean(xf * xf, axis=-1, keepdims=True) + EPS)
    return (y * g.astype(jnp.float32)).astype(x.dtype)


def rotary(x, positions):
    dr = x.shape[-1]
    inv = ROPE_BASE ** (-jnp.arange(0, dr, 2, dtype=jnp.float32) / dr)
    ang = positions.astype(jnp.float32)[..., None] * inv
    cos = jnp.cos(ang)[:, :, None, :]
    sin = jnp.sin(ang)[:, :, None, :]
    xf = x.astype(jnp.float32)
    x1, x2 = xf[..., : dr // 2], xf[..., dr // 2:]
    return jnp.concatenate([x1 * cos - x2 * sin, x1 * sin + x2 * cos], axis=-1).astype(x.dtype)


def causal_dwconv(u, w, b):
    k = w.shape[0]
    out = lax.conv_general_dilated(
        u, w[:, None, :].astype(u.dtype), window_strides=(1,), padding=[(k - 1, 0)],
        dimension_numbers=("NWC", "WIO", "NWC"), feature_group_count=u.shape[-1])
    return out + b


def ssd_mixer(h, w_in, conv_w, conv_b, dt_bias, a_log, d_skip, norm_g, w_out):
    f32 = jnp.float32
    bsz, s, _ = h.shape
    nc = s // CHUNK
    G, R, P, N = SSD_GROUPS, SSD_HEADS // SSD_GROUPS, SSD_HEAD_DIM, SSD_STATE
    zxbcdt = h @ w_in
    z, xbc, dt = jnp.split(zxbcdt, [SSD_D_INNER, SSD_D_INNER + SSD_CONV_DIM], axis=-1)
    xbc = jax.nn.silu(causal_dwconv(xbc, conv_w, conv_b))
    xs, bm, cm = jnp.split(xbc, [SSD_D_INNER, SSD_D_INNER + G * N], axis=-1)
    dt = jax.nn.softplus(dt.astype(f32) + dt_bias.astype(f32))
    a_head = -jnp.exp(a_log.astype(f32)).reshape(G, R)
    xs = xs.astype(f32).reshape(bsz, nc, CHUNK, G, R, P)
    dtc = dt.reshape(bsz, nc, CHUNK, G, R)
    bm = bm.astype(f32).reshape(bsz, nc, CHUNK, G, N)
    cm = cm.astype(f32).reshape(bsz, nc, CHUNK, G, N)
    a_cum = jnp.cumsum(dtc * a_head, axis=2)
    xdt = xs * dtc[..., None]
    idx = jnp.arange(CHUNK)
    tril = (idx[:, None] >= idx[None, :])[None, None, :, :, None, None]
    seg = a_cum[:, :, :, None] - a_cum[:, :, None, :]
    decay = jnp.exp(jnp.where(tril, seg, -jnp.inf))
    cb = jnp.einsum("bclgn,bcsgn->bclsg", cm, bm)
    y_diag = jnp.einsum("bclsgr,bcsgrp->bclgrp", cb[..., None] * decay, xdt)

    def step(state, inp):
        a_c, x_c, b_c, c_c = inp
        y_off = jnp.einsum("blgn,bgrpn->blgrp", c_c, state) * jnp.exp(a_c)[..., None]
        to_end = jnp.exp(a_c[:, -1:] - a_c)
        new = state * jnp.exp(a_c[:, -1])[..., None, None] + jnp.einsum(
            "blgn,blgrp->bgrpn", b_c, x_c * to_end[..., None])
        return new, y_off

    init = jnp.zeros((bsz, G, R, P, N), f32)
    _, y_off = lax.scan(step, init, (jnp.moveaxis(a_cum, 1, 0), jnp.moveaxis(xdt, 1, 0),
                                     jnp.moveaxis(bm, 1, 0), jnp.moveaxis(cm, 1, 0)))
    y = y_diag + jnp.moveaxis(y_off, 0, 1) + xs * d_skip.astype(f32).reshape(G, R)[:, :, None]
    y = y.reshape(bsz, s, SSD_D_INNER)
    yg = (y * jax.nn.silu(z.astype(f32))).reshape(bsz, s, G, SSD_D_INNER // G)
    yg = yg * lax.rsqrt(jnp.mean(yg * yg, axis=-1, keepdims=True) + EPS)
    y = yg.reshape(bsz, s, SSD_D_INNER) * norm_g.astype(f32)
    return y.astype(h.dtype) @ w_out


def mla_mixer(h, positions, w_in, q_norm_g, kv_norm_g, w_uq, w_ukv,
              q_nope_g, q_rope_g, k_nope_g, k_rope_g, w_out):
    f32 = jnp.float32
    bsz, s, _ = h.shape
    H = MLA_HEADS
    lat = h @ w_in
    cq, ckv, k_pe = jnp.split(lat, [MLA_Q_RANK, MLA_Q_RANK + MLA_KV_RANK], axis=-1)
    q = (rms_norm(cq, q_norm_g) @ w_uq).reshape(bsz, s, H, MLA_QK)
    kv = (rms_norm(ckv, kv_norm_g) @ w_ukv).reshape(bsz, s, H, MLA_NOPE + MLA_V)
    k_nope, v = jnp.split(kv, [MLA_NOPE], axis=-1)
    q_nope, q_pe = jnp.split(q, [MLA_NOPE], axis=-1)
    q_nope = rms_norm(q_nope, q_nope_g)
    q_pe = rotary(rms_norm(q_pe, q_rope_g), positions)
    k_nope = rms_norm(k_nope, k_nope_g)
    k_pe = rotary(rms_norm(k_pe, k_rope_g)[:, :, None, :], positions)
    q = jnp.concatenate([q_nope, q_pe], axis=-1)
    k = jnp.concatenate([k_nope, jnp.broadcast_to(k_pe, (bsz, s, H, MLA_ROPE))], axis=-1)
    scale = MLA_QK ** -0.5
    nb = s // Q_BLOCK
    qb = jnp.moveaxis(q.reshape(bsz, nb, Q_BLOCK, H, MLA_QK), 1, 0)
    key_chunk = jnp.arange(s) // CHUNK

    def attend(args):
        q_blk, blk = args
        q_chunk = (blk * Q_BLOCK + jnp.arange(Q_BLOCK)) // CHUNK
        sc = jnp.einsum("bqhd,bkhd->bhqk", q_blk, k).astype(f32) * scale
        mask = key_chunk[None, :] <= q_chunk[:, None]
        p = jax.nn.softmax(jnp.where(mask, sc, -jnp.inf), axis=-1).astype(v.dtype)
        return jnp.einsum("bhqk,bkhd->bqhd", p, v)

    o = lax.map(attend, (qb, jnp.arange(nb)))
    o = jnp.moveaxis(o, 0, 1).reshape(bsz, s, H * MLA_V)
    return o @ w_out


def retention_mixer(h, positions, w_in, norm_g, w_out):
    f32 = jnp.float32
    bsz, s, _ = h.shape
    H, dk, dv = RET_HEADS, RET_QK, RET_V
    nc = s // CHUNK
    proj = h @ w_in
    q, k, v, g = jnp.split(proj, [H * dk, 2 * H * dk, 2 * H * dk + H * dv], axis=-1)
    q = rotary(q.reshape(bsz, s, H, dk), positions).astype(f32)
    k = rotary(k.reshape(bsz, s, H, dk), positions).astype(f32) * (dk ** -0.5)
    v = v.reshape(bsz, s, H, dv).astype(f32)
    log_gamma = jnp.log1p(-jnp.exp2(-5.0 - jnp.arange(H, dtype=f32)))
    idx = jnp.arange(CHUNK, dtype=f32)
    qc = q.reshape(bsz, nc, CHUNK, H, dk)
    kc = k.reshape(bsz, nc, CHUNK, H, dk)
    vc = v.reshape(bsz, nc, CHUNK, H, dv)
    d_intra = jnp.exp(log_gamma[:, None, None] * jnp.abs(idx[:, None] - idx[None, :]))
    sc = jnp.einsum("bclhd,bcshd->bchls", qc, kc) * d_intra
    y_intra = jnp.einsum("bchls,bcshe->bclhe", sc, vc)
    q_decay = jnp.exp(log_gamma[None, :] * (idx[:, None] + 1.0))
    k_decay = jnp.exp(log_gamma[None, :] * (CHUNK - 1.0 - idx[:, None]))
    chunk_decay = jnp.exp(log_gamma * CHUNK)

    def step(state, inp):
        q_c, k_c, v_c = inp
        y = jnp.einsum("blhd,bhde->blhe", q_c * q_decay[:, :, None], state)
        new = state * chunk_decay[:, None, None] + jnp.einsum(
            "blhd,blhe->bhde", k_c * k_decay[:, :, None], v_c)
        return new, y

    init = jnp.zeros((bsz, H, dk, dv), f32)
    _, y_inter = lax.scan(step, init, (jnp.moveaxis(qc, 1, 0), jnp.moveaxis(kc, 1, 0),
                                       jnp.moveaxis(vc, 1, 0)))
    y = (y_intra + jnp.moveaxis(y_inter, 0, 1)).reshape(bsz, s, H, dv)
    y = rms_norm(y, norm_g.reshape(H, dv)).reshape(bsz, s, H * dv)
    y = jax.nn.silu(g.astype(f32)) * y
    return y.astype(h.dtype) @ w_out


def swiglu(h, w_in, w_out):
    gt, up = jnp.split(h @ w_in, 2, axis=-1)
    return (jax.nn.silu(gt) * up) @ w_out


def moe_swiglu(h, router_w, w_in, w_out):
    f32 = jnp.float32
    bsz, s, d = h.shape
    t = bsz * s
    xt = h.reshape(t, d)
    logits = (xt @ router_w).astype(f32)
    top_logit, top_idx = lax.top_k(logits, TOP_K)
    gates = jax.nn.softmax(top_logit, axis=-1)
    e_flat = top_idx.reshape(-1)
    tok_flat = jnp.repeat(jnp.arange(t, dtype=jnp.int32), TOP_K)
    g_flat = gates.reshape(-1)
    order = jnp.argsort(e_flat)
    e_sorted = e_flat[order]
    counts = jnp.bincount(e_flat, length=N_EXPERTS)
    padded = (counts + MOE_BLOCK - 1) // MOE_BLOCK * MOE_BLOCK
    start = jnp.cumsum(counts) - counts
    pend = jnp.cumsum(padded)
    pstart = pend - padded
    dest = pstart[e_sorted] + (jnp.arange(t * TOP_K) - start[e_sorted])
    n_blocks = -(-(t * TOP_K) // MOE_BLOCK) + N_EXPERTS
    cap = n_blocks * MOE_BLOCK
    slot_tok = jnp.zeros((cap,), jnp.int32).at[dest].set(tok_flat[order])
    slot_gate = jnp.zeros((cap,), f32).at[dest].set(g_flat[order])
    block_exp = jnp.clip(jnp.searchsorted(pend, jnp.arange(n_blocks) * MOE_BLOCK, side="right"),
                         0, N_EXPERTS - 1)

    def expert_block(args):
        toks, eid = args
        xb = xt[toks]
        gt, up = jnp.split(xb @ w_in[eid], 2, axis=-1)
        return (jax.nn.silu(gt) * up) @ w_out[eid]

    yb = lax.map(expert_block, (slot_tok.reshape(n_blocks, MOE_BLOCK), block_exp))
    y = jnp.zeros((t, d), f32).at[slot_tok].add(yb.reshape(cap, d).astype(f32) * slot_gate[:, None])
    return y.astype(h.dtype).reshape(bsz, s, d)


def setup_inputs(seed: int = 0) -> dict:
    key = jax.random.key(seed)
    ks = iter(jax.random.split(key, 48))

    def nrm(shape, scale):
        return jax.random.normal(next(ks), shape, jnp.float32) * scale

    def gain(shape):
        return 1.0 + nrm(shape, 0.02)

    D = D_MODEL
    x = nrm((BATCH, SEQ, D), 1.0)
    c = nrm((BATCH, D), 1.0)
    offset = jax.random.randint(next(ks), (BATCH,), 0, 4096, dtype=jnp.int32)
    positions = offset[:, None] + jnp.arange(SEQ, dtype=jnp.int32)[None, :]
    dt = jnp.exp(jax.random.uniform(next(ks), (N_SSD, SSD_HEADS), jnp.float32,
                                    math.log(1e-3), math.log(1e-1)))
    ssd_dt_bias = dt + jnp.log(-jnp.expm1(-dt))
    ssd_a_log = jnp.log(jax.random.uniform(next(ks), (N_SSD, SSD_HEADS), jnp.float32, 1.0, 16.0))
    return {
        "x": x,
        "c": c,
        "positions": positions,
        "ada_w": nrm((DEPTH, D, 6 * D), 0.3 * D ** -0.5),
        "ada_b": nrm((DEPTH, 6 * D), 0.02),
        "norm1_g": gain((DEPTH, D)),
        "norm2_g": gain((DEPTH, D)),
        "ssd_w_in": nrm((N_SSD, D, SSD_IN_DIM), D ** -0.5),
        "ssd_conv_w": nrm((N_SSD, SSD_CONV, SSD_CONV_DIM), SSD_CONV ** -0.5),
        "ssd_conv_b": nrm((N_SSD, SSD_CONV_DIM), 0.02),
        "ssd_dt_bias": ssd_dt_bias,
        "ssd_a_log": ssd_a_log,
        "ssd_d": gain((N_SSD, SSD_HEADS)),
        "ssd_norm_g": gain((N_SSD, SSD_D_INNER)),
        "ssd_w_out": nrm((N_SSD, SSD_D_INNER, D), SSD_D_INNER ** -0.5),
        "mla_w_in": nrm((N_MLA, D, MLA_IN_DIM), D ** -0.5),
        "mla_q_norm_g": gain((N_MLA, MLA_Q_RANK)),
        "mla_kv_norm_g": gain((N_MLA, MLA_KV_RANK)),
        "mla_w_uq": nrm((N_MLA, MLA_Q_RANK, MLA_HEADS * MLA_QK), MLA_Q_RANK ** -0.5),
        "mla_w_ukv": nrm((N_MLA, MLA_KV_RANK, MLA_HEADS * (MLA_NOPE + MLA_V)), MLA_KV_RANK ** -0.5),
        "mla_q_nope_g": gain((N_MLA, MLA_NOPE)),
        "mla_q_rope_g": gain((N_MLA, MLA_ROPE)),
        "mla_k_nope_g": gain((N_MLA, MLA_NOPE)),
        "mla_k_rope_g": gain((N_MLA, MLA_ROPE)),
        "mla_w_out": nrm((N_MLA, MLA_HEADS * MLA_V, D), (MLA_HEADS * MLA_V) ** -0.5),
        "ret_w_in": nrm((N_RET, D, RET_IN_DIM), D ** -0.5),
        "ret_norm_g": gain((N_RET, RET_HEADS * RET_V)),
        "ret_w_out": nrm((N_RET, RET_HEADS * RET_V, D), (RET_HEADS * RET_V) ** -0.5),
        "ffn_w_in": nrm((N_DENSE, D, 2 * FFN_DIM), D ** -0.5),
        "ffn_w_out": nrm((N_DENSE, FFN_DIM, D), FFN_DIM ** -0.5),
        "router_w": nrm((N_MOE, D, N_EXPERTS), D ** -0.5),
        "moe_w_in": nrm((N_MOE, N_EXPERTS, D, 2 * EXPERT_DIM), D ** -0.5),
        "moe_w_out": nrm((N_MOE, N_EXPERTS, EXPERT_DIM, D), EXPERT_DIM ** -0.5),
    }


def reference(x, c, positions, ada_w, ada_b, norm1_g, norm2_g,
              ssd_w_in, ssd_conv_w, ssd_conv_b, ssd_dt_bias, ssd_a_log, ssd_d, ssd_norm_g, ssd_w_out,
              mla_w_in, mla_q_norm_g, mla_kv_norm_g, mla_w_uq, mla_w_ukv,
              mla_q_nope_g, mla_q_rope_g, mla_k_nope_g, mla_k_rope_g, mla_w_out,
              ret_w_in, ret_norm_g, ret_w_out,
              ffn_w_in, ffn_w_out, router_w, moe_w_in, moe_w_out):
    for i in range(DEPTH):
        mod = c @ ada_w[i] + ada_b[i]
        shift1, scale1, gate1, shift2, scale2, gate2 = jnp.split(mod[:, None, :], 6, axis=-1)
        h = rms_norm(x, norm1_g[i]) * (1.0 + scale1) + shift1
        kind, j = i % N_MIXERS, i // N_MIXERS
        if kind == 0:
            y = ssd_mixer(h, ssd_w_in[j], ssd_conv_w[j], ssd_conv_b[j], ssd_dt_bias[j],
                          ssd_a_log[j], ssd_d[j], ssd_norm_g[j], ssd_w_out[j])
        elif kind == 1:
            y = mla_mixer(h, positions, mla_w_in[j], mla_q_norm_g[j], mla_kv_norm_g[j],
                          mla_w_uq[j], mla_w_ukv[j], mla_q_nope_g[j], mla_q_rope_g[j],
                          mla_k_nope_g[j], mla_k_rope_g[j], mla_w_out[j])
        else:
            y = retention_mixer(h, positions, ret_w_in[j], ret_norm_g[j], ret_w_out[j])
        x = x + gate1 * y
        h = rms_norm(x, norm2_g[i]) * (1.0 + scale2) + shift2
        if i % 2 == 0:
            y = swiglu(h, ffn_w_in[i // 2], ffn_w_out[i // 2])
        else:
            y = moe_swiglu(h, router_w[i // 2], moe_w_in[i // 2], moe_w_out[i // 2])
        x = x + gate2 * y
    return x
```

```python
import functools
import math

import numpy as np
import jax
import jax.numpy as jnp
from jax import lax
from jax.experimental import pallas as pl
from jax.experimental.pallas import tpu as pltpu

F32 = jnp.float32
BF16 = jnp.bfloat16
EPS = 1e-6
LANES = 128
MIB = 1024 * 1024

CHUNK = 64
SSD_HEAD_DIM = 64
SSD_GROUPS = 4
SSD_STATE = 128
SSD_CONV = 4
MLA_HEADS = 8
MLA_NOPE = 128
MLA_ROPE = 64
MLA_V = 128
MLA_Q_RANK = 256
MLA_KV_RANK = 256
ROPE_BASE = 10000.0
RET_QK = 256
RET_V = 512
N_MIXERS = 3
TOP_K = 2

SSD_L = 128
RET_L = 256
ATT_T = 512
MOE_ROWS = 512
FFN_TF = 256


def _cparams(sem, vmem_mib):
    return pltpu.CompilerParams(dimension_semantics=sem, vmem_limit_bytes=vmem_mib * MIB)


def _silu(v):
    return v * jax.nn.sigmoid(v)


def _norm_mod(x, g, shift, scale):
    y = x * lax.rsqrt(jnp.mean(x * x, axis=-1, keepdims=True) + EPS)
    return (y * g) * (1.0 + scale) + shift


def _ada_kernel(cb_ref, w_ref, b_ref, o_ref):
    nb = cb_ref.shape[0]
    tn = w_ref.shape[2]
    for jn in range(tn // LANES):
        cols = slice(jn * LANES, (jn + 1) * LANES)
        w = w_ref[0, :, cols]
        for b in range(nb):
            s = jnp.sum(w * cb_ref[b], axis=0, keepdims=True)
            o_ref[0, b:b + 1, cols] = s + b_ref[0, :, cols]


def _ada_mod(c, ada_w, ada_b, tn=1024):
    depth, d, n = ada_w.shape
    bsz = c.shape[0]
    cb = jnp.broadcast_to(c[:, :, None], (bsz, d, LANES))
    return pl.pallas_call(
        _ada_kernel,
        grid=(depth, n // tn),
        in_specs=[pl.BlockSpec((bsz, d, LANES), lambda l, j: (0, 0, 0)),
                  pl.BlockSpec((1, d, tn), lambda l, j: (l, 0, j)),
                  pl.BlockSpec((1, 1, tn), lambda l, j: (l, 0, j))],
        out_specs=pl.BlockSpec((1, bsz, tn), lambda l, j: (l, 0, j)),
        out_shape=jax.ShapeDtypeStruct((depth, bsz, n), F32),
        compiler_params=_cparams(("parallel", "parallel"), 32),
        name="ada_mod",
    )(cb, ada_w, ada_b.reshape(depth, 1, n))


def _nm_matmul_kernel(x_ref, g_ref, sh_ref, sc_ref, w_ref, o_ref, h_ref):
    @pl.when(pl.program_id(1) == 0)
    def _():
        h_ref[...] = _norm_mod(x_ref[...], g_ref[...], sh_ref[0], sc_ref[0]).astype(BF16)

    o_ref[...] = jnp.dot(h_ref[...], w_ref[...], preferred_element_type=F32).astype(o_ref.dtype)


def _norm_mod_matmul(x, g, shift, scale, w, out_dtype, tm=1024, tn=None):
    t, d = x.shape
    n = w.shape[1]
    tn = n if tn is None else tn
    tiles_per_batch = t // shift.shape[0] // tm
    mod_spec = pl.BlockSpec((1, 1, d), lambda i, j: (i // tiles_per_batch, 0, 0))
    return pl.pallas_call(
        _nm_matmul_kernel,
        grid=(t // tm, n // tn),
        in_specs=[pl.BlockSpec((tm, d), lambda i, j: (i, 0)),
                  pl.BlockSpec((1, d), lambda i, j: (0, 0)),
                  mod_spec, mod_spec,
                  pl.BlockSpec((d, tn), lambda i, j: (0, j))],
        out_specs=pl.BlockSpec((tm, tn), lambda i, j: (i, j)),
        out_shape=jax.ShapeDtypeStruct((t, n), out_dtype),
        scratch_shapes=[pltpu.VMEM((tm, d), BF16)],
        compiler_params=_cparams(("parallel", "arbitrary"), 48),
        name="norm_mod_matmul",
    )(x, g.reshape(1, d), shift, scale, w)


def _mm_res_kernel(y_ref, w_ref, x_ref, gate_ref, o_ref):
    o_ref[...] = x_ref[...] + gate_ref[0] * jnp.dot(y_ref[...], w_ref[...], preferred_element_type=F32)


def _matmul_residual(y, w, x, gate, tm=512):
    t, k = y.shape
    d = w.shape[1]
    tiles_per_batch = t // gate.shape[0] // tm
    return pl.pallas_call(
        _mm_res_kernel,
        grid=(t // tm,),
        in_specs=[pl.BlockSpec((tm, k), lambda i: (i, 0)),
                  pl.BlockSpec((k, d), lambda i: (0, 0)),
                  pl.BlockSpec((tm, d), lambda i: (i, 0)),
                  pl.BlockSpec((1, 1, d), lambda i: (i // tiles_per_batch, 0, 0))],
        out_specs=pl.BlockSpec((tm, d), lambda i: (i, 0)),
        out_shape=jax.ShapeDtypeStruct((t, d), F32),
        compiler_params=_cparams(("parallel",), 48),
        name="matmul_residual",
    )(y, w, x, gate)


def _swiglu_accumulate(h_ref, w_in, w_out, acc_ref, n_chunks, tf):
    acc_ref[...] = jnp.zeros_like(acc_ref)

    def body(f, carry):
        gu = jnp.dot(h_ref[...], w_in(f), preferred_element_type=F32)
        act = (_silu(gu[:, :tf]) * gu[:, tf:]).astype(BF16)
        acc_ref[...] += jnp.dot(act, w_out(f), preferred_element_type=F32)
        return carry

    lax.fori_loop(0, n_chunks, body, 0)


def _chunk_swiglu_weights(w_in, w_out, tf):
    *lead, d, f2 = w_in.shape
    f = f2 // 2
    nf = f // tf
    wg = w_in[..., :f].reshape(*lead, d, nf, tf)
    wu = w_in[..., f:].reshape(*lead, d, nf, tf)
    wi = jnp.concatenate([wg, wu], axis=-1).astype(BF16)
    wi = jnp.moveaxis(wi, -2, -3)
    wo = w_out.astype(BF16).reshape(*lead, nf, tf, d)
    return wi, wo


def _ffn_kernel(x_ref, g_ref, sh_ref, sc_ref, gate_ref, wi_ref, wo_ref, o_ref, h_ref, acc_ref, *, tf):
    x = x_ref[...]
    h_ref[...] = _norm_mod(x, g_ref[...], sh_ref[0], sc_ref[0]).astype(BF16)
    _swiglu_accumulate(h_ref, lambda f: wi_ref[f], lambda f: wo_ref[f], acc_ref, wi_ref.shape[0], tf)
    o_ref[...] = x + gate_ref[0] * acc_ref[...]


def _ffn(x, g, shift, scale, gate, w_in, w_out, tm=512, tf=FFN_TF):
    t, d = x.shape
    wi, wo = _chunk_swiglu_weights(w_in, w_out, tf)
    nf = wi.shape[0]
    tiles_per_batch = t // gate.shape[0] // tm
    mod_spec = pl.BlockSpec((1, 1, d), lambda i: (i // tiles_per_batch, 0, 0))
    resident = pl.Buffered(1)
    return pl.pallas_call(
        functools.partial(_ffn_kernel, tf=tf),
        grid=(t // tm,),
        in_specs=[pl.BlockSpec((tm, d), lambda i: (i, 0)),
                  pl.BlockSpec((1, d), lambda i: (0, 0)),
                  mod_spec, mod_spec, mod_spec,
                  pl.BlockSpec((nf, d, 2 * tf), lambda i: (0, 0, 0), pipeline_mode=resident),
                  pl.BlockSpec((nf, tf, d), lambda i: (0, 0, 0), pipeline_mode=resident)],
        out_specs=pl.BlockSpec((tm, d), lambda i: (i, 0)),
        out_shape=jax.ShapeDtypeStruct((t, d), F32),
        scratch_shapes=[pltpu.VMEM((tm, d), BF16), pltpu.VMEM((tm, d), F32)],
        compiler_params=_cparams(("parallel",), 48),
        name="ffn",
    )(x, g.reshape(1, d), shift, scale, gate, wi, wo)


def _rope_kernel(pos_ref, inv_ret_ref, inv_mla_ref, rc_ref, rs_ref, mc_ref, ma_ref, mb_ref):
    pos = pos_ref[...]
    ang = pos * inv_ret_ref[...]
    rc_ref[...] = jnp.cos(ang)
    rs_ref[...] = jnp.sin(ang)
    half = MLA_ROPE // 2
    ang = pos * inv_mla_ref[...]
    lane = lax.broadcasted_iota(jnp.int32, ang.shape, 1)
    c = jnp.cos(ang)
    s = jnp.sin(ang)
    mc_ref[...] = jnp.where(lane < MLA_ROPE, c, 0.0)
    ma_ref[...] = jnp.where(lane < half, -s, 0.0)
    mb_ref[...] = jnp.where((lane >= half) & (lane < MLA_ROPE), s, 0.0)


def _rope_tables(positions, tm=1024):
    t = positions.size
    pos = positions.astype(F32).reshape(t, 1)
    inv_ret = ROPE_BASE ** (-jnp.arange(0, RET_QK, 2, dtype=F32) / RET_QK)
    inv_half = ROPE_BASE ** (-jnp.arange(0, MLA_ROPE, 2, dtype=F32) / MLA_ROPE)
    inv_mla = jnp.concatenate([inv_half, inv_half, jnp.zeros((LANES - MLA_ROPE,), F32)])
    row = pl.BlockSpec((tm, LANES), lambda i: (i, 0))
    const = pl.BlockSpec((1, LANES), lambda i: (0, 0))
    out = jax.ShapeDtypeStruct((t, LANES), F32)
    return pl.pallas_call(
        _rope_kernel,
        grid=(t // tm,),
        in_specs=[pl.BlockSpec((tm, 1), lambda i: (i, 0)), const, const],
        out_specs=[row] * 5,
        out_shape=[out] * 5,
        compiler_params=_cparams(("parallel",), 32),
        name="rope_tables",
    )(pos, inv_ret.reshape(1, LANES), inv_mla.reshape(1, LANES))


def _softplus(v):
    return jnp.maximum(v, 0.0) + jnp.log1p(jnp.exp(-jnp.abs(v)))


def _ssd_kernel(zx_ref, dt_ref, cw_ref, cb_ref, dtb_ref, ah_ref, dsk_ref, ng_ref, o_ref,
                ext_ref, act_ref, st_ref, acum_ref, acumT_ref, dtT_ref, wcol_ref, y_ref, xw_ref, dec_ref,
                *, d_inner, groups, n_state, head_dim):
    L = zx_ref.shape[0]
    conv_dim = act_ref.shape[1]
    heads_per_group = d_inner // groups // head_dim
    gw = d_inner // groups
    pairs = gw // LANES

    @pl.when(pl.program_id(1) == 0)
    def _():
        ext_ref[0:8, :] = jnp.zeros((8, conv_dim), F32)
        st_ref[...] = jnp.zeros_like(st_ref)

    ext_ref[8:8 + L, :] = zx_ref[:, d_inner:d_inner + conv_dim].astype(F32)
    conv = cb_ref[...]
    for k in range(SSD_CONV):
        conv = conv + cw_ref[k:k + 1, :] * ext_ref[5 + k:5 + k + L, :]
    act_ref[...] = _silu(conv)
    ext_ref[0:8, :] = ext_ref[L:L + 8, :]

    dt = _softplus(dt_ref[...] + dtb_ref[...])
    a = dt * ah_ref[...]
    row = lax.broadcasted_iota(jnp.int32, (L, L), 0)
    col = lax.broadcasted_iota(jnp.int32, (L, L), 1)
    tril = row >= col
    a_cum = jnp.dot(tril.astype(F32), a, precision=lax.Precision.HIGHEST, preferred_element_type=F32)
    a_last = a_cum[L - 1:L, :]
    acum_ref[...] = a_cum
    acumT_ref[...] = a_cum.T
    dtT_ref[...] = dt.T
    wcol_ref[...] = dt * jnp.exp(a_last - a_cum)

    lane = lax.broadcasted_iota(jnp.int32, (L, LANES), 1)
    lane_lo = lane < head_dim
    lane_lo_row = lane_lo[0:1, :]

    for g in range(groups):
        b_g = act_ref[:, d_inner + g * n_state:d_inner + (g + 1) * n_state].astype(BF16)
        c_g = act_ref[:, d_inner + (groups + g) * n_state:d_inner + (groups + g + 1) * n_state].astype(BF16)
        cb = lax.dot_general(c_g, b_g, (((1,), (1,)), ((), ())), preferred_element_type=F32)
        y_off = jnp.dot(c_g, st_ref[g].astype(BF16), preferred_element_type=F32)
        for p in range(pairs):
            cols = slice(g * gw + p * LANES, g * gw + (p + 1) * LANES)
            xs_pair = act_ref[:, cols]
            xs_bf = xs_pair.astype(BF16)
            res, e_l, w_l = [], [], []
            for hh in range(LANES // head_dim):
                h = g * heads_per_group + p * (LANES // head_dim) + hh
                a_l = jnp.broadcast_to(acum_ref[:, h:h + 1], (L, L))
                m = jnp.where(tril, jnp.exp(a_l - acumT_ref[h:h + 1, :]), 0.0) * cb * dtT_ref[h:h + 1, :]
                res.append(jnp.dot(m.astype(BF16), xs_bf, preferred_element_type=F32))
                e_l.append(jnp.exp(jnp.broadcast_to(acum_ref[:, h:h + 1], (L, LANES))))
                w_l.append(jnp.broadcast_to(wcol_ref[:, h:h + 1], (L, LANES)))
            yo = y_off[:, p * LANES:(p + 1) * LANES]
            y_ref[:, cols] = (jnp.where(lane_lo, res[0] + e_l[0] * yo, res[1] + e_l[1] * yo)
                              + xs_pair * dsk_ref[:, cols])
            xw_ref[:, p * LANES:(p + 1) * LANES] = (xs_pair * jnp.where(lane_lo, w_l[0], w_l[1])).astype(BF16)
            dec_ref[:, p * LANES:(p + 1) * LANES] = jnp.where(lane_lo_row, e_l[0][L - 1:L, :], e_l[1][L - 1:L, :])
        upd = lax.dot_general(b_g, xw_ref[...], (((0,), (0,)), ((), ())), preferred_element_type=F32)
        st_ref[g] = st_ref[g] * dec_ref[...] + upd

    for g in range(groups):
        cols = slice(g * gw, (g + 1) * gw)
        yg = y_ref[:, cols] * _silu(zx_ref[:, cols].astype(F32))
        yg = yg * lax.rsqrt(jnp.mean(yg * yg, axis=-1, keepdims=True) + EPS)
        o_ref[:, cols] = (yg * ng_ref[:, cols]).astype(o_ref.dtype)


def _ssd_scan(zx, dt_raw, conv_w, conv_b, dt_bias, a_log, d_skip, norm_g, bsz, L=SSD_L):
    t = zx.shape[0]
    s = t // bsz
    heads = dt_bias.shape[0]
    d_inner = heads * SSD_HEAD_DIM
    conv_dim = conv_w.shape[1]
    gw = d_inner // SSD_GROUPS
    steps = s // L

    def pad_heads(v):
        return jnp.zeros((1, LANES), F32).at[0, :heads].set(v.astype(F32))

    a_head = pad_heads(-jnp.exp(a_log.astype(F32)))
    d_exp = jnp.repeat(d_skip.astype(F32), SSD_HEAD_DIM).reshape(1, d_inner)
    row = lambda w: pl.BlockSpec((L, w), lambda b, j: (b * steps + j, 0))
    const = lambda r, w: pl.BlockSpec((r, w), lambda b, j: (0, 0))
    kern = functools.partial(_ssd_kernel, d_inner=d_inner, groups=SSD_GROUPS, n_state=SSD_STATE,
                             head_dim=SSD_HEAD_DIM)
    return pl.pallas_call(
        kern,
        grid=(bsz, steps),
        in_specs=[row(zx.shape[1]), row(LANES), const(SSD_CONV, conv_dim), const(1, conv_dim),
                  const(1, LANES), const(1, LANES), const(1, d_inner), const(1, d_inner)],
        out_specs=row(d_inner),
        out_shape=jax.ShapeDtypeStruct((t, d_inner), BF16),
        scratch_shapes=[pltpu.VMEM((L + 8, conv_dim), F32),
                        pltpu.VMEM((L, conv_dim), F32),
                        pltpu.VMEM((SSD_GROUPS, SSD_STATE, gw), F32),
                        pltpu.VMEM((L, LANES), F32),
                        pltpu.VMEM((LANES, L), F32),
                        pltpu.VMEM((LANES, L), F32),
                        pltpu.VMEM((L, LANES), F32),
                        pltpu.VMEM((L, d_inner), F32),
                        pltpu.VMEM((L, gw), BF16),
                        pltpu.VMEM((1, gw), F32)],
        compiler_params=_cparams(("parallel", "arbitrary"), 48),
        name="ssd_scan",
    )(zx, dt_raw, conv_w.astype(F32), conv_b.astype(F32).reshape(1, conv_dim), pad_heads(dt_bias), a_head,
      d_exp, norm_g.astype(F32).reshape(1, d_inner))


def _rms_rows(v, n):
    return v * lax.rsqrt(jnp.sum(v * v, axis=-1, keepdims=True) / n + EPS)


def _mla_prep_kernel(lat_ref, qg_ref, kvg_ref, wqn_ref, wqp_ref, wkv_ref, gqn_ref, gqp_ref, gkn_ref, gkp_ref,
                     mc_ref, ma_ref, mb_ref, q_ref, k_ref, v_ref, *, scale):
    lat = lat_ref[...]
    cq = lat[:, :MLA_Q_RANK]
    ckv = lat[:, MLA_Q_RANK:MLA_Q_RANK + MLA_KV_RANK]
    kpe = lat[:, MLA_Q_RANK + MLA_KV_RANK:]
    qn = (_rms_rows(cq, MLA_Q_RANK) * qg_ref[...]).astype(BF16)
    kvn = (_rms_rows(ckv, MLA_KV_RANK) * kvg_ref[...]).astype(BF16)
    q_nope = jnp.dot(qn, wqn_ref[...], preferred_element_type=F32)
    q_pe = jnp.dot(qn, wqp_ref[...], preferred_element_type=F32)
    kv = jnp.dot(kvn, wkv_ref[...], preferred_element_type=F32)
    mc, ma, mb = mc_ref[...], ma_ref[...], mb_ref[...]

    def rotate(v):
        return v * mc + pltpu.roll(v, LANES - MLA_ROPE // 2, 1) * ma + pltpu.roll(v, MLA_ROPE // 2, 1) * mb

    k_pe = rotate(_rms_rows(kpe, MLA_ROPE) * gkp_ref[...]).astype(BF16)
    for h in range(MLA_HEADS):
        qn_h = _rms_rows(q_nope[:, h * LANES:(h + 1) * LANES], MLA_NOPE) * gqn_ref[...]
        qp_h = rotate(_rms_rows(q_pe[:, h * LANES:(h + 1) * LANES], MLA_ROPE) * gqp_ref[...])
        q_ref[0, h, :, 0:LANES] = (qn_h * scale).astype(BF16)
        q_ref[0, h, :, LANES:2 * LANES] = (qp_h * scale).astype(BF16)
        kn_h = _rms_rows(kv[:, 2 * h * LANES:(2 * h + 1) * LANES], MLA_NOPE) * gkn_ref[...]
        k_ref[0, h, :, 0:LANES] = kn_h.astype(BF16)
        k_ref[0, h, :, LANES:2 * LANES] = k_pe
        v_ref[0, h] = kv[:, (2 * h + 1) * LANES:(2 * h + 2) * LANES].astype(BF16)


def _pad_lanes(v, n=LANES):
    return jnp.zeros((1, n), F32).at[0, :v.shape[0]].set(v.astype(F32))


def _mla_prep(lat, tables, q_norm_g, kv_norm_g, w_uq, w_ukv, q_nope_g, q_rope_g, k_nope_g, k_rope_g, bsz, tm=256):
    t = lat.shape[0]
    s = t // bsz
    steps = s // tm
    hh = MLA_HEADS
    w_uq = w_uq.reshape(MLA_Q_RANK, hh, MLA_NOPE + MLA_ROPE)
    wqn = w_uq[:, :, :MLA_NOPE].reshape(MLA_Q_RANK, hh * MLA_NOPE).astype(BF16)
    wqp = jnp.pad(w_uq[:, :, MLA_NOPE:], ((0, 0), (0, 0), (0, LANES - MLA_ROPE)))
    wqp = wqp.reshape(MLA_Q_RANK, hh * LANES).astype(BF16)
    wkv = w_ukv.astype(BF16)
    mc, ma, mb = tables
    row = pl.BlockSpec((tm, LANES), lambda b, i: (b * steps + i, 0))
    const = lambda r, w: pl.BlockSpec((r, w), lambda b, i: (0, 0))
    head_out = lambda w: pl.BlockSpec((1, hh, tm, w), lambda b, i: (b, 0, i, 0))
    return pl.pallas_call(
        functools.partial(_mla_prep_kernel, scale=(MLA_NOPE + MLA_ROPE) ** -0.5),
        grid=(bsz, steps),
        in_specs=[pl.BlockSpec((tm, lat.shape[1]), lambda b, i: (b * steps + i, 0)),
                  const(1, MLA_Q_RANK), const(1, MLA_KV_RANK),
                  const(MLA_Q_RANK, hh * LANES), const(MLA_Q_RANK, hh * LANES), const(MLA_KV_RANK, hh * 2 * LANES),
                  const(1, LANES), const(1, LANES), const(1, LANES), const(1, LANES),
                  row, row, row],
        out_specs=[head_out(2 * LANES), head_out(2 * LANES), head_out(LANES)],
        out_shape=[jax.ShapeDtypeStruct((bsz, hh, s, 2 * LANES), BF16),
                   jax.ShapeDtypeStruct((bsz, hh, s, 2 * LANES), BF16),
                   jax.ShapeDtypeStruct((bsz, hh, s, LANES), BF16)],
        compiler_params=_cparams(("parallel", "parallel"), 48),
        name="mla_prep",
    )(lat, _pad_lanes(q_norm_g, MLA_Q_RANK), _pad_lanes(kv_norm_g, MLA_KV_RANK), wqn, wqp, wkv,
      _pad_lanes(q_nope_g), _pad_lanes(q_rope_g), _pad_lanes(k_nope_g), _pad_lanes(k_rope_g), mc, ma, mb)


def _flash_kernel(qi_ref, kj_ref, q_ref, k_ref, v_ref, o_ref, m_ref, l_ref, acc_ref):
    p = pl.program_id(2)
    i = qi_ref[p]
    j = kj_ref[p]

    @pl.when(j == 0)
    def _():
        m_ref[...] = jnp.full_like(m_ref, -jnp.inf)
        l_ref[...] = jnp.zeros_like(l_ref)
        acc_ref[...] = jnp.zeros_like(acc_ref)

    def update(masked):
        s = lax.dot_general(q_ref[0, 0], k_ref[0, 0], (((1,), (1,)), ((), ())), preferred_element_type=F32)
        if masked:
            row = lax.broadcasted_iota(jnp.int32, s.shape, 0) // CHUNK
            col = lax.broadcasted_iota(jnp.int32, s.shape, 1) // CHUNK
            s = jnp.where(col <= row, s, -jnp.inf)
        m_prev = m_ref[...]
        m_new = jnp.maximum(m_prev, jnp.max(s, axis=-1, keepdims=True))
        alpha = jnp.exp(m_prev - m_new)
        pr = jnp.exp(s - m_new)
        l_ref[...] = alpha * l_ref[...] + jnp.sum(pr, axis=-1, keepdims=True)
        acc_ref[...] = alpha * acc_ref[...] + jnp.dot(pr.astype(BF16), v_ref[0, 0], preferred_element_type=F32)
        m_ref[...] = m_new

    @pl.when(j < i)
    def _():
        update(False)

    @pl.when(j == i)
    def _():
        update(True)
        o_ref[0] = (acc_ref[...] / l_ref[...]).astype(o_ref.dtype)


def _flash_attention(q, k, v, tile=ATT_T):
    bsz, hh, s, dq = q.shape
    dv = v.shape[-1]
    nt = s // tile
    qi = np.concatenate([np.full((i + 1,), i, np.int32) for i in range(nt)])
    kj = np.concatenate([np.arange(i + 1, dtype=np.int32) for i in range(nt)])
    grid_spec = pltpu.PrefetchScalarGridSpec(
        num_scalar_prefetch=2,
        grid=(bsz, hh, qi.size),
        in_specs=[pl.BlockSpec((1, 1, tile, dq), lambda b, h, p, qi, kj: (b, h, qi[p], 0)),
                  pl.BlockSpec((1, 1, tile, dq), lambda b, h, p, qi, kj: (b, h, kj[p], 0)),
                  pl.BlockSpec((1, 1, tile, dv), lambda b, h, p, qi, kj: (b, h, kj[p], 0))],
        out_specs=pl.BlockSpec((1, tile, dv), lambda b, h, p, qi, kj: (b, qi[p], h)),
        scratch_shapes=[pltpu.VMEM((tile, 1), F32), pltpu.VMEM((tile, 1), F32), pltpu.VMEM((tile, dv), F32)],
    )
    return pl.pallas_call(
        _flash_kernel,
        grid_spec=grid_spec,
        out_shape=jax.ShapeDtypeStruct((bsz, s, hh * dv), BF16),
        compiler_params=_cparams(("parallel", "parallel", "arbitrary"), 48),
        name="flash_attention",
    )(jnp.asarray(qi), jnp.asarray(kj), q, k, v)


def _ret_kernel(p_ref, cos_ref, sin_ref, ng_ref, o_ref, st_ref, dm_ref, qd_ref, kd_ref, *, heads, log_gamma):
    L = p_ref.shape[0]
    dk, dv = RET_QK, RET_V
    half = dk // 2

    @pl.when(pl.program_id(1) == 0)
    def _():
        st_ref[...] = jnp.zeros_like(st_ref)
        row = lax.broadcasted_iota(jnp.int32, (L, L), 0)
        col = lax.broadcasted_iota(jnp.int32, (L, L), 1)
        visible = (col // CHUNK) <= (row // CHUNK)
        dist = jnp.abs(row - col).astype(F32)
        pos = lax.broadcasted_iota(jnp.int32, (L, LANES), 0).astype(F32)
        for h in range(heads):
            dm_ref[h] = jnp.where(visible, jnp.exp(log_gamma[h] * dist), 0.0)
            qd_ref[h] = jnp.exp(log_gamma[h] * (pos + 1.0))
            kd_ref[h] = jnp.exp(log_gamma[h] * (L - 1.0 - pos))

    c = cos_ref[...]
    s = sin_ref[...]
    k_scale = dk ** -0.5
    for h in range(heads):
        q1 = p_ref[:, h * dk:h * dk + half].astype(F32)
        q2 = p_ref[:, h * dk + half:(h + 1) * dk].astype(F32)
        k1 = p_ref[:, (heads + h) * dk:(heads + h) * dk + half].astype(F32)
        k2 = p_ref[:, (heads + h) * dk + half:(heads + h + 1) * dk].astype(F32)
        v = p_ref[:, 2 * heads * dk + h * dv:2 * heads * dk + (h + 1) * dv]
        gate = p_ref[:, 2 * heads * dk + heads * dv + h * dv:2 * heads * dk + heads * dv + (h + 1) * dv]
        qr1, qr2 = q1 * c - q2 * s, q1 * s + q2 * c
        kr1, kr2 = (k1 * c - k2 * s) * k_scale, (k1 * s + k2 * c) * k_scale
        qd, kd = qd_ref[h], kd_ref[h]
        q_bf = jnp.concatenate([qr1, qr2], axis=-1).astype(BF16)
        k_bf = jnp.concatenate([kr1, kr2], axis=-1).astype(BF16)
        q_in = jnp.concatenate([qr1 * qd, qr2 * qd], axis=-1).astype(BF16)
        k_out = jnp.concatenate([kr1 * kd, kr2 * kd], axis=-1).astype(BF16)
        sc = lax.dot_general(q_bf, k_bf, (((1,), (1,)), ((), ())), preferred_element_type=F32) * dm_ref[h]
        y = (jnp.dot(sc.astype(BF16), v, preferred_element_type=F32)
             + jnp.dot(q_in, st_ref[h].astype(BF16), preferred_element_type=F32))
        upd = lax.dot_general(k_out, v, (((0,), (0,)), ((), ())), preferred_element_type=F32)
        st_ref[h] = st_ref[h] * math.exp(log_gamma[h] * L) + upd
        y = y * lax.rsqrt(jnp.mean(y * y, axis=-1, keepdims=True) + EPS) * ng_ref[:, h * dv:(h + 1) * dv]
        o_ref[:, h * dv:(h + 1) * dv] = (_silu(gate.astype(F32)) * y).astype(o_ref.dtype)


def _retention(proj, cos, sin, norm_g, bsz, heads, L=RET_L):
    t = proj.shape[0]
    steps = t // bsz // L
    log_gamma = tuple(math.log1p(-2.0 ** (-5.0 - h)) for h in range(heads))
    row = lambda w: pl.BlockSpec((L, w), lambda b, j: (b * steps + j, 0))
    return pl.pallas_call(
        functools.partial(_ret_kernel, heads=heads, log_gamma=log_gamma),
        grid=(bsz, steps),
        in_specs=[row(proj.shape[1]), row(LANES), row(LANES),
                  pl.BlockSpec((1, heads * RET_V), lambda b, j: (0, 0))],
        out_specs=row(heads * RET_V),
        out_shape=jax.ShapeDtypeStruct((t, heads * RET_V), BF16),
        scratch_shapes=[pltpu.VMEM((heads, RET_QK, RET_V), F32),
                        pltpu.VMEM((heads, L, L), F32),
                        pltpu.VMEM((heads, L, LANES), F32),
                        pltpu.VMEM((heads, L, LANES), F32)],
        compiler_params=_cparams(("parallel", "arbitrary"), 48),
        name="retention",
    )(proj, cos, sin, norm_g.astype(F32).reshape(1, heads * RET_V))


def _router_kernel(x_ref, g_ref, sh_ref, sc_ref, rw_ref, h_ref, idx_ref, gates_ref, *, n_experts):
    h = _norm_mod(x_ref[...], g_ref[...], sh_ref[0], sc_ref[0])
    h_ref[...] = h
    logits = jnp.dot(h, rw_ref[...], precision=lax.Precision.HIGHEST, preferred_element_type=F32)
    lane = lax.broadcasted_iota(jnp.int32, logits.shape, 1).astype(F32)
    lg = jnp.where(lane < n_experts, logits, -jnp.inf)
    m1 = jnp.max(lg, axis=-1, keepdims=True)
    i1 = jnp.min(jnp.where(lg == m1, lane, float(LANES)), axis=-1, keepdims=True)
    lg2 = jnp.where(lane == i1, -jnp.inf, lg)
    m2 = jnp.max(lg2, axis=-1, keepdims=True)
    i2 = jnp.min(jnp.where(lg2 == m2, lane, float(LANES)), axis=-1, keepdims=True)
    e2 = jnp.exp(m2 - m1)
    idx_ref[:, 0:1] = i1.astype(jnp.int32)
    idx_ref[:, 1:2] = i2.astype(jnp.int32)
    gates_ref[:, 0:1] = 1.0 / (1.0 + e2)
    gates_ref[:, 1:2] = e2 / (1.0 + e2)


def _router(x, g, shift, scale, router_w, tm=512):
    t, d = x.shape
    n_experts = router_w.shape[1]
    rw = jnp.zeros((d, LANES), F32).at[:, :n_experts].set(router_w.astype(F32))
    tiles_per_batch = t // shift.shape[0] // tm
    mod_spec = pl.BlockSpec((1, 1, d), lambda i: (i // tiles_per_batch, 0, 0))
    return pl.pallas_call(
        functools.partial(_router_kernel, n_experts=n_experts),
        grid=(t // tm,),
        in_specs=[pl.BlockSpec((tm, d), lambda i: (i, 0)),
                  pl.BlockSpec((1, d), lambda i: (0, 0)),
                  mod_spec, mod_spec,
                  pl.BlockSpec((d, LANES), lambda i: (0, 0))],
        out_specs=[pl.BlockSpec((tm, d), lambda i: (i, 0)),
                   pl.BlockSpec((tm, TOP_K), lambda i: (i, 0)),
                   pl.BlockSpec((tm, TOP_K), lambda i: (i, 0))],
        out_shape=[jax.ShapeDtypeStruct((t, d), F32),
                   jax.ShapeDtypeStruct((t, TOP_K), jnp.int32),
                   jax.ShapeDtypeStruct((t, TOP_K), F32)],
        compiler_params=_cparams(("parallel",), 48),
        name="router",
    )(x, g.reshape(1, d), shift, scale, rw)


def _row_copy(src_ref, dst_ref, src_row, dst_row, sem):
    return pltpu.make_async_copy(src_ref.at[pl.ds(src_row, 1)], dst_ref.at[pl.ds(dst_row, 1)], sem)


def _gather_kernel(idx_ref, src_ref, o_ref, sem):
    rows = o_ref.shape[0]

    def start(r, carry):
        _row_copy(src_ref, o_ref, idx_ref[0, 0, r], r, sem).start()
        return carry

    def wait(r, carry):
        _row_copy(src_ref, o_ref, 0, r, sem).wait()
        return carry

    lax.fori_loop(0, rows, start, 0)
    lax.fori_loop(0, rows, wait, 0)


def _gather_rows(src, idx, rows=MOE_ROWS):
    n = idx.shape[0]
    d = src.shape[1]
    return pl.pallas_call(
        _gather_kernel,
        grid=(n // rows,),
        in_specs=[pl.BlockSpec((1, 1, rows), lambda i: (i, 0, 0), memory_space=pltpu.SMEM),
                  pl.BlockSpec(memory_space=pl.ANY)],
        out_specs=pl.BlockSpec((rows, d), lambda i: (i, 0)),
        out_shape=jax.ShapeDtypeStruct((n, d), src.dtype),
        scratch_shapes=[pltpu.SemaphoreType.DMA(())],
        compiler_params=_cparams(("arbitrary",), 32),
        name="gather_rows",
    )(idx.reshape(n // rows, 1, rows), src)


def _expert_kernel(bexp_ref, bvalid_ref, xg_ref, wi_ref, wo_ref, o_ref, h_ref, acc_ref, *, tf):
    b = pl.program_id(0)

    @pl.when(bvalid_ref[b] != 0)
    def _():
        h_ref[...] = xg_ref[...].astype(BF16)
        _swiglu_accumulate(h_ref, lambda f: wi_ref[0, f], lambda f: wo_ref[0, f], acc_ref, wi_ref.shape[1], tf)
        o_ref[...] = acc_ref[...]

    @pl.when(bvalid_ref[b] == 0)
    def _():
        o_ref[...] = jnp.zeros_like(o_ref)


def _expert_ffn(xg, block_exp, block_valid, wi, wo, rows=MOE_ROWS, tf=FFN_TF):
    cap, d = xg.shape
    _, nf, _, tf2 = wi.shape
    grid_spec = pltpu.PrefetchScalarGridSpec(
        num_scalar_prefetch=2,
        grid=(cap // rows,),
        in_specs=[pl.BlockSpec((rows, d), lambda b, be, bv: (b, 0)),
                  pl.BlockSpec((1, nf, d, tf2), lambda b, be, bv: (be[b], 0, 0, 0)),
                  pl.BlockSpec((1, nf, tf, d), lambda b, be, bv: (be[b], 0, 0, 0),
                               pipeline_mode=pl.Buffered(1))],
        out_specs=pl.BlockSpec((rows, d), lambda b, be, bv: (b, 0)),
        scratch_shapes=[pltpu.VMEM((rows, d), BF16), pltpu.VMEM((rows, d), F32)],
    )
    return pl.pallas_call(
        functools.partial(_expert_kernel, tf=tf),
        grid_spec=grid_spec,
        out_shape=jax.ShapeDtypeStruct((cap, d), F32),
        compiler_params=_cparams(("arbitrary",), 56),
        name="expert_ffn",
    )(block_exp, block_valid, xg, wi, wo)


def _combine_kernel(d0_ref, d1_ref, gates_ref, x_ref, gate_ref, yb_ref, o_ref, buf0, buf1, sem):
    rows = o_ref.shape[0]

    def start(r, carry):
        _row_copy(yb_ref, buf0, d0_ref[0, 0, r], r, sem.at[0]).start()
        _row_copy(yb_ref, buf1, d1_ref[0, 0, r], r, sem.at[1]).start()
        return carry

    def wait(r, carry):
        _row_copy(yb_ref, buf0, 0, r, sem.at[0]).wait()
        _row_copy(yb_ref, buf1, 0, r, sem.at[1]).wait()
        return carry

    lax.fori_loop(0, rows, start, 0)
    lax.fori_loop(0, rows, wait, 0)
    gts = gates_ref[...]
    y = gts[:, 0:1] * buf0[...] + gts[:, 1:2] * buf1[...]
    o_ref[...] = x_ref[...] + gate_ref[0] * y


def _moe_combine(x, gate, yb, dest, gates, tm=256):
    t, d = x.shape
    tiles_per_batch = t // gate.shape[0] // tm
    slot = pl.BlockSpec((1, 1, tm), lambda i: (i, 0, 0), memory_space=pltpu.SMEM)
    return pl.pallas_call(
        _combine_kernel,
        grid=(t // tm,),
        in_specs=[slot, slot,
                  pl.BlockSpec((tm, TOP_K), lambda i: (i, 0)),
                  pl.BlockSpec((tm, d), lambda i: (i, 0)),
                  pl.BlockSpec((1, 1, d), lambda i: (i // tiles_per_batch, 0, 0)),
                  pl.BlockSpec(memory_space=pl.ANY)],
        out_specs=pl.BlockSpec((tm, d), lambda i: (i, 0)),
        out_shape=jax.ShapeDtypeStruct((t, d), F32),
        scratch_shapes=[pltpu.VMEM((tm, d), F32), pltpu.VMEM((tm, d), F32), pltpu.SemaphoreType.DMA((2,))],
        compiler_params=_cparams(("arbitrary",), 32),
        name="moe_combine",
    )(dest[:, 0].reshape(t // tm, 1, tm), dest[:, 1].reshape(t // tm, 1, tm), gates, x, gate, yb)


def _route(top_idx, n_experts, rows):
    t = top_idx.shape[0]
    n_assign = t * TOP_K
    e_flat = top_idx.reshape(-1)
    onehot = (e_flat[:, None] == jnp.arange(n_experts, dtype=jnp.int32)[None, :]).astype(jnp.int32)
    csum = jnp.cumsum(onehot, axis=0)
    rank = jnp.sum((csum - onehot) * onehot, axis=-1)
    counts = csum[-1]
    padded = (counts + rows - 1) // rows * rows
    pend = jnp.cumsum(padded)
    pstart = pend - padded
    dest = (pstart[e_flat] + rank).astype(jnp.int32)
    n_blocks = n_assign // rows + n_experts
    tok_flat = jnp.repeat(jnp.arange(t, dtype=jnp.int32), TOP_K)
    slot_tok = jnp.zeros((n_blocks * rows,), jnp.int32).at[dest].set(tok_flat)
    block_start = jnp.arange(n_blocks, dtype=jnp.int32) * rows
    block_exp = jnp.clip(jnp.searchsorted(pend, block_start, side="right"), 0, n_experts - 1).astype(jnp.int32)
    block_valid = (block_start < pend[-1]).astype(jnp.int32)
    return slot_tok, dest.reshape(t, TOP_K), block_exp, block_valid


def _moe(x, g, shift, scale, gate, router_w, w_in, w_out):
    n_experts = router_w.shape[1]
    h, top_idx, gates = _router(x, g, shift, scale, router_w)
    slot_tok, dest, block_exp, block_valid = _route(top_idx, n_experts, MOE_ROWS)
    xg = _gather_rows(h, slot_tok)
    wi, wo = _chunk_swiglu_weights(w_in, w_out, FFN_TF)
    yb = _expert_ffn(xg, block_exp, block_valid, wi, wo)
    return _moe_combine(x, gate, yb, dest, gates)


def _ssd_layer(x, g, shift, scale, gate, bsz, w_in, conv_w, conv_b, dt_bias, a_log, d_skip, norm_g, w_out):
    heads = dt_bias.shape[0]
    d_inner = heads * SSD_HEAD_DIM
    conv_dim = conv_w.shape[1]
    n_main = d_inner + conv_dim
    w_main = w_in[:, :n_main].astype(BF16)
    w_dt = jnp.zeros((w_in.shape[0], LANES), BF16).at[:, :heads].set(w_in[:, n_main:].astype(BF16))
    zx = _norm_mod_matmul(x, g, shift, scale, w_main, BF16, tn=1024)
    dt_raw = _norm_mod_matmul(x, g, shift, scale, w_dt, F32)
    y = _ssd_scan(zx, dt_raw, conv_w, conv_b, dt_bias, a_log, d_skip, norm_g, bsz)
    return _matmul_residual(y, w_out.astype(BF16), x, gate)


def _mla_layer(x, g, shift, scale, gate, bsz, mla_tables, w_in, q_norm_g, kv_norm_g, w_uq, w_ukv,
               q_nope_g, q_rope_g, k_nope_g, k_rope_g, w_out):
    lat_dim = MLA_Q_RANK + MLA_KV_RANK + LANES
    w_lat = jnp.zeros((w_in.shape[0], lat_dim), BF16).at[:, :w_in.shape[1]].set(w_in.astype(BF16))
    lat = _norm_mod_matmul(x, g, shift, scale, w_lat, F32)
    q, k, v = _mla_prep(lat, mla_tables, q_norm_g, kv_norm_g, w_uq, w_ukv, q_nope_g, q_rope_g, k_nope_g,
                        k_rope_g, bsz)
    o = _flash_attention(q, k, v)
    return _matmul_residual(o.reshape(x.shape[0], -1), w_out.astype(BF16), x, gate)


def _ret_layer(x, g, shift, scale, gate, bsz, ret_tables, w_in, norm_g, w_out):
    heads = w_out.shape[0] // RET_V
    proj = _norm_mod_matmul(x, g, shift, scale, w_in.astype(BF16), BF16, tn=1024)
    y = _retention(proj, ret_tables[0], ret_tables[1], norm_g, bsz, heads)
    return _matmul_residual(y, w_out.astype(BF16), x, gate)


def kernel(x, c, positions, ada_w, ada_b, norm1_g, norm2_g, ssd_w_in, ssd_conv_w, ssd_conv_b, ssd_dt_bias, ssd_a_log, ssd_d, ssd_norm_g, ssd_w_out, mla_w_in, mla_q_norm_g, mla_kv_norm_g, mla_w_uq, mla_w_ukv, mla_q_nope_g, mla_q_rope_g, mla_k_nope_g, mla_k_rope_g, mla_w_out, ret_w_in, ret_norm_g, ret_w_out, ffn_w_in, ffn_w_out, router_w, moe_w_in, moe_w_out):
    bsz, s, d = x.shape
    depth = ada_w.shape[0]
    mod = _ada_mod(c, ada_w, ada_b).reshape(depth, bsz, 6, 1, d)
    rc, rs, mc, ma, mb = _rope_tables(positions)
    xt = x.reshape(bsz * s, d)
    for i in range(depth):
        shift1, scale1, gate1, shift2, scale2, gate2 = (mod[i, :, m] for m in range(6))
        kind, j = i % N_MIXERS, i // N_MIXERS
        if kind == 0:
            xt = _ssd_layer(xt, norm1_g[i], shift1, scale1, gate1, bsz, ssd_w_in[j], ssd_conv_w[j], ssd_conv_b[j],
                            ssd_dt_bias[j], ssd_a_log[j], ssd_d[j], ssd_norm_g[j], ssd_w_out[j])
        elif kind == 1:
            xt = _mla_layer(xt, norm1_g[i], shift1, scale1, gate1, bsz, (mc, ma, mb), mla_w_in[j],
                            mla_q_norm_g[j], mla_kv_norm_g[j], mla_w_uq[j], mla_w_ukv[j], mla_q_nope_g[j],
                            mla_q_rope_g[j], mla_k_nope_g[j], mla_k_rope_g[j], mla_w_out[j])
        else:
            xt = _ret_layer(xt, norm1_g[i], shift1, scale1, gate1, bsz, (rc, rs), ret_w_in[j], ret_norm_g[j],
                            ret_w_out[j])
        if i % 2 == 0:
            xt = _ffn(xt, norm2_g[i], shift2, scale2, gate2, ffn_w_in[i // 2], ffn_w_out[i // 2])
        else:
            xt = _moe(xt, norm2_g[i], shift2, scale2, gate2, router_w[i // 2], moe_w_in[i // 2], moe_w_out[i // 2])
    return xt.reshape(bsz, s, d)
```

```python
import functools
import math

import numpy as np
import jax
import jax.numpy as jnp
from jax import lax
from jax.experimental import pallas as pl
from jax.experimental.pallas import tpu as pltpu

F32 = jnp.float32
BF16 = jnp.bfloat16
EPS = 1e-6
LANES = 128
MIB = 1024 * 1024

CHUNK = 64
SSD_HEAD_DIM = 64
SSD_GROUPS = 4
SSD_STATE = 128
SSD_CONV = 4
MLA_HEADS = 8
MLA_NOPE = 128
MLA_ROPE = 64
MLA_V = 128
MLA_Q_RANK = 256
MLA_KV_RANK = 256
ROPE_BASE = 10000.0
RET_QK = 256
RET_V = 512
N_MIXERS = 3
TOP_K = 2

SSD_L = 128
RET_L = 256
ATT_TK = 512
DMA_UNROLL = 16
MOE_ROWS = 512
FFN_TF = 256


def _cparams(sem, vmem_mib):
    return pltpu.CompilerParams(dimension_semantics=sem, vmem_limit_bytes=vmem_mib * MIB)


def _silu(v):
    return v * jax.nn.sigmoid(v)


def _norm_mod(x, g, shift, scale):
    y = x * lax.rsqrt(jnp.mean(x * x, axis=-1, keepdims=True) + EPS)
    return (y * g) * (1.0 + scale) + shift


def _ada_kernel(cb_ref, w_ref, b_ref, o_ref):
    nb = cb_ref.shape[0]
    tn = w_ref.shape[2]
    for jn in range(tn // LANES):
        cols = slice(jn * LANES, (jn + 1) * LANES)
        w = w_ref[0, :, cols]
        for b in range(nb):
            s = jnp.sum(w * cb_ref[b], axis=0, keepdims=True)
            o_ref[0, b:b + 1, cols] = s + b_ref[0, :, cols]


def _ada_mod(c, ada_w, ada_b, tn=1024):
    depth, d, n = ada_w.shape
    bsz = c.shape[0]
    cb = jnp.broadcast_to(c[:, :, None], (bsz, d, LANES))
    return pl.pallas_call(
        _ada_kernel,
        grid=(depth, n // tn),
        in_specs=[pl.BlockSpec((bsz, d, LANES), lambda l, j: (0, 0, 0)),
                  pl.BlockSpec((1, d, tn), lambda l, j: (l, 0, j)),
                  pl.BlockSpec((1, 1, tn), lambda l, j: (l, 0, j))],
        out_specs=pl.BlockSpec((1, bsz, tn), lambda l, j: (l, 0, j)),
        out_shape=jax.ShapeDtypeStruct((depth, bsz, n), F32),
        compiler_params=_cparams(("parallel", "parallel"), 32),
        name="ada_mod",
    )(cb, ada_w, ada_b.reshape(depth, 1, n))


def _nm_matmul_kernel(x_ref, g_ref, sh_ref, sc_ref, w_ref, o_ref, h_ref):
    @pl.when(pl.program_id(1) == 0)
    def _():
        h_ref[...] = _norm_mod(x_ref[...], g_ref[...], sh_ref[0], sc_ref[0]).astype(BF16)

    o_ref[...] = jnp.dot(h_ref[...], w_ref[...], preferred_element_type=F32).astype(o_ref.dtype)


def _norm_mod_matmul(x, g, shift, scale, w, out_dtype, tm=1024, tn=None):
    t, d = x.shape
    n = w.shape[1]
    tn = n if tn is None else tn
    tiles_per_batch = t // shift.shape[0] // tm
    mod_spec = pl.BlockSpec((1, 1, d), lambda i, j: (i // tiles_per_batch, 0, 0))
    return pl.pallas_call(
        _nm_matmul_kernel,
        grid=(t // tm, n // tn),
        in_specs=[pl.BlockSpec((tm, d), lambda i, j: (i, 0)),
                  pl.BlockSpec((1, d), lambda i, j: (0, 0)),
                  mod_spec, mod_spec,
                  pl.BlockSpec((d, tn), lambda i, j: (0, j))],
        out_specs=pl.BlockSpec((tm, tn), lambda i, j: (i, j)),
        out_shape=jax.ShapeDtypeStruct((t, n), out_dtype),
        scratch_shapes=[pltpu.VMEM((tm, d), BF16)],
        compiler_params=_cparams(("parallel", "arbitrary"), 48),
        name="norm_mod_matmul",
    )(x, g.reshape(1, d), shift, scale, w)


def _mm_res_kernel(y_ref, w_ref, x_ref, gate_ref, o_ref):
    o_ref[...] = x_ref[...] + gate_ref[0] * jnp.dot(y_ref[...], w_ref[...], preferred_element_type=F32)


def _matmul_residual(y, w, x, gate, tm=512):
    t, k = y.shape
    d = w.shape[1]
    tiles_per_batch = t // gate.shape[0] // tm
    return pl.pallas_call(
        _mm_res_kernel,
        grid=(t // tm,),
        in_specs=[pl.BlockSpec((tm, k), lambda i: (i, 0)),
                  pl.BlockSpec((k, d), lambda i: (0, 0)),
                  pl.BlockSpec((tm, d), lambda i: (i, 0)),
                  pl.BlockSpec((1, 1, d), lambda i: (i // tiles_per_batch, 0, 0))],
        out_specs=pl.BlockSpec((tm, d), lambda i: (i, 0)),
        out_shape=jax.ShapeDtypeStruct((t, d), F32),
        compiler_params=_cparams(("parallel",), 48),
        name="matmul_residual",
    )(y, w, x, gate)


def _swiglu_accumulate(h_ref, wi_ref, wo_ref, acc_ref, tf):
    hidden = wo_ref.shape[0]
    acc_ref[...] = jnp.zeros_like(acc_ref)

    def body(f, carry):
        lo = pl.multiple_of(f * tf, tf)
        h = h_ref[...]
        gt = jnp.dot(h, wi_ref[:, pl.ds(lo, tf)], preferred_element_type=F32)
        up = jnp.dot(h, wi_ref[:, pl.ds(hidden + lo, tf)], preferred_element_type=F32)
        act = (_silu(gt) * up).astype(BF16)
        acc_ref[...] += jnp.dot(act, wo_ref[pl.ds(lo, tf), :], preferred_element_type=F32)
        return carry

    lax.fori_loop(0, hidden // tf, body, 0)


def _ffn_kernel(x_ref, g_ref, sh_ref, sc_ref, gate_ref, wi_ref, wo_ref, o_ref, h_ref, acc_ref, *, tf):
    x = x_ref[...]
    h_ref[...] = _norm_mod(x, g_ref[...], sh_ref[0], sc_ref[0]).astype(BF16)
    _swiglu_accumulate(h_ref, wi_ref, wo_ref, acc_ref, tf)
    o_ref[...] = x + gate_ref[0] * acc_ref[...]


def _ffn(x, g, shift, scale, gate, w_in, w_out, tm=512, tf=FFN_TF):
    t, d = x.shape
    wi, wo = w_in.astype(BF16), w_out.astype(BF16)
    hidden = wo.shape[0]
    tiles_per_batch = t // gate.shape[0] // tm
    mod_spec = pl.BlockSpec((1, 1, d), lambda i: (i // tiles_per_batch, 0, 0))
    resident = pl.Buffered(1)
    return pl.pallas_call(
        functools.partial(_ffn_kernel, tf=tf),
        grid=(t // tm,),
        in_specs=[pl.BlockSpec((tm, d), lambda i: (i, 0)),
                  pl.BlockSpec((1, d), lambda i: (0, 0)),
                  mod_spec, mod_spec, mod_spec,
                  pl.BlockSpec((d, 2 * hidden), lambda i: (0, 0), pipeline_mode=resident),
                  pl.BlockSpec((hidden, d), lambda i: (0, 0), pipeline_mode=resident)],
        out_specs=pl.BlockSpec((tm, d), lambda i: (i, 0)),
        out_shape=jax.ShapeDtypeStruct((t, d), F32),
        scratch_shapes=[pltpu.VMEM((tm, d), BF16), pltpu.VMEM((tm, d), F32)],
        compiler_params=_cparams(("parallel",), 48),
        name="ffn",
    )(x, g.reshape(1, d), shift, scale, gate, wi, wo)


def _rope_kernel(pos_ref, inv_ret_ref, inv_mla_ref, rc_ref, rs_ref, mc_ref, ma_ref, mb_ref):
    pos = pos_ref[...]
    ang = pos * inv_ret_ref[...]
    rc_ref[...] = jnp.cos(ang)
    rs_ref[...] = jnp.sin(ang)
    half = MLA_ROPE // 2
    ang = pos * inv_mla_ref[...]
    lane = lax.broadcasted_iota(jnp.int32, ang.shape, 1)
    c = jnp.cos(ang)
    s = jnp.sin(ang)
    mc_ref[...] = jnp.where(lane < MLA_ROPE, c, 0.0)
    ma_ref[...] = jnp.where(lane < half, -s, 0.0)
    mb_ref[...] = jnp.where((lane >= half) & (lane < MLA_ROPE), s, 0.0)


def _rope_tables(positions, tm=1024):
    t = positions.size
    pos = positions.astype(F32).reshape(t, 1)
    inv_ret = ROPE_BASE ** (-jnp.arange(0, RET_QK, 2, dtype=F32) / RET_QK)
    inv_half = ROPE_BASE ** (-jnp.arange(0, MLA_ROPE, 2, dtype=F32) / MLA_ROPE)
    inv_mla = jnp.concatenate([inv_half, inv_half, jnp.zeros((LANES - MLA_ROPE,), F32)])
    row = pl.BlockSpec((tm, LANES), lambda i: (i, 0))
    const = pl.BlockSpec((1, LANES), lambda i: (0, 0))
    out = jax.ShapeDtypeStruct((t, LANES), F32)
    return pl.pallas_call(
        _rope_kernel,
        grid=(t // tm,),
        in_specs=[pl.BlockSpec((tm, 1), lambda i: (i, 0)), const, const],
        out_specs=[row] * 5,
        out_shape=[out] * 5,
        compiler_params=_cparams(("parallel",), 32),
        name="rope_tables",
    )(pos, inv_ret.reshape(1, LANES), inv_mla.reshape(1, LANES))


def _softplus(v):
    return jnp.maximum(v, 0.0) + jnp.log1p(jnp.exp(-jnp.abs(v)))


def _ssd_kernel(zx_ref, dt_ref, cw_ref, cb_ref, dtb_ref, ah_ref, dsk_ref, ng_ref, o_ref,
                ext_ref, act_ref, st_ref, acum_ref, acumT_ref, dtT_ref, wcol_ref, y_ref, xw_ref, dec_ref,
                *, d_inner, groups, n_state, head_dim):
    L = zx_ref.shape[0]
    conv_dim = act_ref.shape[1]
    heads_per_group = d_inner // groups // head_dim
    gw = d_inner // groups
    pairs = gw // LANES

    @pl.when(pl.program_id(1) == 0)
    def _():
        ext_ref[0:8, :] = jnp.zeros((8, conv_dim), F32)
        st_ref[...] = jnp.zeros_like(st_ref)

    ext_ref[8:8 + L, :] = zx_ref[:, d_inner:d_inner + conv_dim].astype(F32)
    conv = cb_ref[...]
    for k in range(SSD_CONV):
        conv = conv + cw_ref[k:k + 1, :] * ext_ref[5 + k:5 + k + L, :]
    act_ref[...] = _silu(conv)
    ext_ref[0:8, :] = ext_ref[L:L + 8, :]

    dt = _softplus(dt_ref[...] + dtb_ref[...])
    a = dt * ah_ref[...]
    row = lax.broadcasted_iota(jnp.int32, (L, L), 0)
    col = lax.broadcasted_iota(jnp.int32, (L, L), 1)
    tril = row >= col
    a_cum = jnp.dot(tril.astype(F32), a, precision=lax.Precision.HIGHEST, preferred_element_type=F32)
    a_last = a_cum[L - 1:L, :]
    acum_ref[...] = a_cum
    acumT_ref[...] = a_cum.T
    dtT_ref[...] = dt.T
    wcol_ref[...] = dt * jnp.exp(a_last - a_cum)

    lane = lax.broadcasted_iota(jnp.int32, (L, LANES), 1)
    lane_lo = lane < head_dim
    lane_lo_row = lane_lo[0:1, :]

    for g in range(groups):
        b_g = act_ref[:, d_inner + g * n_state:d_inner + (g + 1) * n_state].astype(BF16)
        c_g = act_ref[:, d_inner + (groups + g) * n_state:d_inner + (groups + g + 1) * n_state].astype(BF16)
        cb = lax.dot_general(c_g, b_g, (((1,), (1,)), ((), ())), preferred_element_type=F32)
        y_off = jnp.dot(c_g, st_ref[g].astype(BF16), preferred_element_type=F32)
        for p in range(pairs):
            cols = slice(g * gw + p * LANES, g * gw + (p + 1) * LANES)
            xs_pair = act_ref[:, cols]
            xs_bf = xs_pair.astype(BF16)
            res, e_l, w_l = [], [], []
            for hh in range(LANES // head_dim):
                h = g * heads_per_group + p * (LANES // head_dim) + hh
                a_l = jnp.broadcast_to(acum_ref[:, h:h + 1], (L, L))
                m = jnp.where(tril, jnp.exp(a_l - acumT_ref[h:h + 1, :]), 0.0) * cb * dtT_ref[h:h + 1, :]
                res.append(jnp.dot(m.astype(BF16), xs_bf, preferred_element_type=F32))
                e_l.append(jnp.exp(jnp.broadcast_to(acum_ref[:, h:h + 1], (L, LANES))))
                w_l.append(jnp.broadcast_to(wcol_ref[:, h:h + 1], (L, LANES)))
            yo = y_off[:, p * LANES:(p + 1) * LANES]
            y_ref[:, cols] = (jnp.where(lane_lo, res[0] + e_l[0] * yo, res[1] + e_l[1] * yo)
                              + xs_pair * dsk_ref[:, cols])
            xw_ref[:, p * LANES:(p + 1) * LANES] = (xs_pair * jnp.where(lane_lo, w_l[0], w_l[1])).astype(BF16)
            dec_ref[:, p * LANES:(p + 1) * LANES] = jnp.where(lane_lo_row, e_l[0][L - 1:L, :], e_l[1][L - 1:L, :])
        upd = lax.dot_general(b_g, xw_ref[...], (((0,), (0,)), ((), ())), preferred_element_type=F32)
        st_ref[g] = st_ref[g] * dec_ref[...] + upd

    for g in range(groups):
        cols = slice(g * gw, (g + 1) * gw)
        yg = y_ref[:, cols] * _silu(zx_ref[:, cols].astype(F32))
        yg = yg * lax.rsqrt(jnp.mean(yg * yg, axis=-1, keepdims=True) + EPS)
        o_ref[:, cols] = (yg * ng_ref[:, cols]).astype(o_ref.dtype)


def _ssd_scan(zx, dt_raw, conv_w, conv_b, dt_bias, a_log, d_skip, norm_g, bsz, L=SSD_L):
    t = zx.shape[0]
    s = t // bsz
    heads = dt_bias.shape[0]
    d_inner = heads * SSD_HEAD_DIM
    conv_dim = conv_w.shape[1]
    gw = d_inner // SSD_GROUPS
    steps = s // L

    def pad_heads(v):
        return jnp.zeros((1, LANES), F32).at[0, :heads].set(v.astype(F32))

    a_head = pad_heads(-jnp.exp(a_log.astype(F32)))
    d_exp = jnp.repeat(d_skip.astype(F32), SSD_HEAD_DIM).reshape(1, d_inner)
    row = lambda w: pl.BlockSpec((L, w), lambda b, j: (b * steps + j, 0))
    const = lambda r, w: pl.BlockSpec((r, w), lambda b, j: (0, 0))
    kern = functools.partial(_ssd_kernel, d_inner=d_inner, groups=SSD_GROUPS, n_state=SSD_STATE,
                             head_dim=SSD_HEAD_DIM)
    return pl.pallas_call(
        kern,
        grid=(bsz, steps),
        in_specs=[row(zx.shape[1]), row(LANES), const(SSD_CONV, conv_dim), const(1, conv_dim),
                  const(1, LANES), const(1, LANES), const(1, d_inner), const(1, d_inner)],
        out_specs=row(d_inner),
        out_shape=jax.ShapeDtypeStruct((t, d_inner), BF16),
        scratch_shapes=[pltpu.VMEM((L + 8, conv_dim), F32),
                        pltpu.VMEM((L, conv_dim), F32),
                        pltpu.VMEM((SSD_GROUPS, SSD_STATE, gw), F32),
                        pltpu.VMEM((L, LANES), F32),
                        pltpu.VMEM((LANES, L), F32),
                        pltpu.VMEM((LANES, L), F32),
                        pltpu.VMEM((L, LANES), F32),
                        pltpu.VMEM((L, d_inner), F32),
                        pltpu.VMEM((L, gw), BF16),
                        pltpu.VMEM((1, gw), F32)],
        compiler_params=_cparams(("parallel", "arbitrary"), 48),
        name="ssd_scan",
    )(zx, dt_raw, conv_w.astype(F32), conv_b.astype(F32).reshape(1, conv_dim), pad_heads(dt_bias), a_head,
      d_exp, norm_g.astype(F32).reshape(1, d_inner))


def _rms_rows(v, n):
    return v * lax.rsqrt(jnp.sum(v * v, axis=-1, keepdims=True) / n + EPS)


def _mla_prep_kernel(lat_ref, qg_ref, kvg_ref, wqn_ref, wqp_ref, wkv_ref, gqn_ref, gqp_ref, gkn_ref, gkp_ref,
                     mc_ref, ma_ref, mb_ref, q_ref, k_ref, v_ref, *, scale):
    lat = lat_ref[...]
    cq = lat[:, :MLA_Q_RANK]
    ckv = lat[:, MLA_Q_RANK:MLA_Q_RANK + MLA_KV_RANK]
    kpe = lat[:, MLA_Q_RANK + MLA_KV_RANK:]
    qn = (_rms_rows(cq, MLA_Q_RANK) * qg_ref[...]).astype(BF16)
    kvn = (_rms_rows(ckv, MLA_KV_RANK) * kvg_ref[...]).astype(BF16)
    q_nope = jnp.dot(qn, wqn_ref[...], preferred_element_type=F32)
    q_pe = jnp.dot(qn, wqp_ref[...], preferred_element_type=F32)
    kv = jnp.dot(kvn, wkv_ref[...], preferred_element_type=F32)
    mc, ma, mb = mc_ref[...], ma_ref[...], mb_ref[...]

    def rotate(v):
        return v * mc + pltpu.roll(v, LANES - MLA_ROPE // 2, 1) * ma + pltpu.roll(v, MLA_ROPE // 2, 1) * mb

    k_pe = rotate(_rms_rows(kpe, MLA_ROPE) * gkp_ref[...]).astype(BF16)
    for h in range(MLA_HEADS):
        qn_h = _rms_rows(q_nope[:, h * LANES:(h + 1) * LANES], MLA_NOPE) * gqn_ref[...]
        qp_h = rotate(_rms_rows(q_pe[:, h * LANES:(h + 1) * LANES], MLA_ROPE) * gqp_ref[...])
        q_ref[0, h, :, 0:LANES] = (qn_h * scale).astype(BF16)
        q_ref[0, h, :, LANES:2 * LANES] = (qp_h * scale).astype(BF16)
        kn_h = _rms_rows(kv[:, 2 * h * LANES:(2 * h + 1) * LANES], MLA_NOPE) * gkn_ref[...]
        k_ref[0, h, :, 0:LANES] = kn_h.astype(BF16)
        k_ref[0, h, :, LANES:2 * LANES] = k_pe
        v_ref[0, h] = kv[:, (2 * h + 1) * LANES:(2 * h + 2) * LANES].astype(BF16)


def _pad_lanes(v, n=LANES):
    return jnp.zeros((1, n), F32).at[0, :v.shape[0]].set(v.astype(F32))


def _mla_prep(lat, tables, q_norm_g, kv_norm_g, w_uq, w_ukv, q_nope_g, q_rope_g, k_nope_g, k_rope_g, bsz, tm=256):
    t = lat.shape[0]
    s = t // bsz
    steps = s // tm
    hh = MLA_HEADS
    w_uq = w_uq.reshape(MLA_Q_RANK, hh, MLA_NOPE + MLA_ROPE)
    wqn = w_uq[:, :, :MLA_NOPE].reshape(MLA_Q_RANK, hh * MLA_NOPE).astype(BF16)
    wqp = jnp.pad(w_uq[:, :, MLA_NOPE:], ((0, 0), (0, 0), (0, LANES - MLA_ROPE)))
    wqp = wqp.reshape(MLA_Q_RANK, hh * LANES).astype(BF16)
    wkv = w_ukv.astype(BF16)
    mc, ma, mb = tables
    row = pl.BlockSpec((tm, LANES), lambda b, i: (b * steps + i, 0))
    const = lambda r, w: pl.BlockSpec((r, w), lambda b, i: (0, 0))
    head_out = lambda w: pl.BlockSpec((1, hh, tm, w), lambda b, i: (b, 0, i, 0))
    return pl.pallas_call(
        functools.partial(_mla_prep_kernel, scale=(MLA_NOPE + MLA_ROPE) ** -0.5),
        grid=(bsz, steps),
        in_specs=[pl.BlockSpec((tm, lat.shape[1]), lambda b, i: (b * steps + i, 0)),
                  const(1, MLA_Q_RANK), const(1, MLA_KV_RANK),
                  const(MLA_Q_RANK, hh * LANES), const(MLA_Q_RANK, hh * LANES), const(MLA_KV_RANK, hh * 2 * LANES),
                  const(1, LANES), const(1, LANES), const(1, LANES), const(1, LANES),
                  row, row, row],
        out_specs=[head_out(2 * LANES), head_out(2 * LANES), head_out(LANES)],
        out_shape=[jax.ShapeDtypeStruct((bsz, hh, s, 2 * LANES), BF16),
                   jax.ShapeDtypeStruct((bsz, hh, s, 2 * LANES), BF16),
                   jax.ShapeDtypeStruct((bsz, hh, s, LANES), BF16)],
        compiler_params=_cparams(("parallel", "parallel"), 48),
        name="mla_prep",
    )(lat, _pad_lanes(q_norm_g, MLA_Q_RANK), _pad_lanes(kv_norm_g, MLA_KV_RANK), wqn, wqp, wkv,
      _pad_lanes(q_nope_g), _pad_lanes(q_rope_g), _pad_lanes(k_nope_g), _pad_lanes(k_rope_g), mc, ma, mb)


def _flash_kernel(q_ref, k_ref, v_ref, o_ref, s_ref, p_ref, m_ref, l_ref, acc_ref, *, tq, tk):
    i = pl.program_id(2)
    n_strips = tq // CHUNK
    chunks_per_tile = tk // CHUNK

    def scores(j, slot):
        kt = k_ref[0, 0, pl.ds(pl.multiple_of(j * tk, tk), tk), :]
        s_ref[slot] = lax.dot_general(q_ref[0, 0], kt, (((1,), (1,)), ((), ())), preferred_element_type=F32)

    def update(j, slot, diag=None):
        lane = lax.broadcasted_iota(jnp.int32, (CHUNK, LANES), 1)
        m_all, l_all = m_ref[...], l_ref[...]
        m_out, l_out, a_out = [], [], []
        for r in range(n_strips):
            rows = slice(r * CHUNK, (r + 1) * CHUNK)
            m_prev, l_prev = m_all[rows], l_all[rows]
            visible = tk if diag is None else min(max((r - diag * chunks_per_tile + 1) * CHUNK, 0), tk)
            n_blocks = -(-visible // LANES)
            if n_blocks < tk // LANES:
                p_ref[rows, n_blocks * LANES:] = jnp.zeros((CHUNK, tk - n_blocks * LANES), BF16)
            if visible == 0:
                m_out.append(m_prev)
                l_out.append(l_prev)
                a_out.append(jnp.ones((CHUNK, LANES), F32))
                continue
            blocks = [s_ref[slot, rows, c * LANES:(c + 1) * LANES] for c in range(n_blocks)]
            if visible % LANES:
                blocks[-1] = jnp.where(lane < visible % LANES, blocks[-1], -jnp.inf)
            row_max = jnp.max(functools.reduce(jnp.maximum, blocks), axis=-1, keepdims=True)
            m_new = jnp.maximum(m_prev, jnp.broadcast_to(row_max, (CHUNK, LANES)))
            alpha = jnp.exp(m_prev - m_new)
            ps = [jnp.exp(b - m_new) for b in blocks]
            for c in range(n_blocks):
                p_ref[rows, c * LANES:(c + 1) * LANES] = ps[c].astype(BF16)
            m_out.append(m_new)
            l_out.append(alpha * l_prev + functools.reduce(jnp.add, ps))
            a_out.append(alpha)
        m_ref[...] = jnp.concatenate(m_out, axis=0)
        l_ref[...] = jnp.concatenate(l_out, axis=0)
        vt = v_ref[0, 0, pl.ds(pl.multiple_of(j * tk, tk), tk), :]
        acc_ref[...] = (jnp.concatenate(a_out, axis=0) * acc_ref[...]
                        + jnp.dot(p_ref[...], vt, preferred_element_type=F32))

    m_ref[...] = jnp.full_like(m_ref, -jnp.inf)
    l_ref[...] = jnp.zeros_like(l_ref)
    acc_ref[...] = jnp.zeros_like(acc_ref)
    scores(0, 0)

    def body(t, carry):
        scores(2 * t + 1, 1)
        update(2 * t, 0)
        scores(2 * t + 2, 0)
        update(2 * t + 1, 1)
        return carry

    lax.fori_loop(0, i, body, 0)
    scores(2 * i + 1, 1)
    update(2 * i, 0, diag=0)
    update(2 * i + 1, 1, diag=1)
    o_ref[0] = (acc_ref[...] / jnp.sum(l_ref[...], axis=-1, keepdims=True)).astype(o_ref.dtype)


def _flash_attention(q, k, v, tk=ATT_TK):
    bsz, hh, s, dq = q.shape
    dv = v.shape[-1]
    tq = 2 * tk
    return pl.pallas_call(
        functools.partial(_flash_kernel, tq=tq, tk=tk),
        grid=(bsz, hh, s // tq),
        in_specs=[pl.BlockSpec((1, 1, tq, dq), lambda b, h, i: (b, h, i, 0)),
                  pl.BlockSpec((1, 1, s, dq), lambda b, h, i: (b, h, 0, 0)),
                  pl.BlockSpec((1, 1, s, dv), lambda b, h, i: (b, h, 0, 0))],
        out_specs=pl.BlockSpec((1, tq, dv), lambda b, h, i: (b, i, h)),
        out_shape=jax.ShapeDtypeStruct((bsz, s, hh * dv), BF16),
        scratch_shapes=[pltpu.VMEM((2, tq, tk), F32),
                        pltpu.VMEM((tq, tk), BF16),
                        pltpu.VMEM((tq, LANES), F32),
                        pltpu.VMEM((tq, LANES), F32),
                        pltpu.VMEM((tq, dv), F32)],
        compiler_params=_cparams(("parallel", "parallel", "arbitrary"), 48),
        name="flash_attention",
    )(q, k, v)


def _ret_kernel(p_ref, cos_ref, sin_ref, ng_ref, o_ref, st_ref, dm_ref, qd_ref, kd_ref, *, heads, log_gamma):
    L = p_ref.shape[0]
    dk, dv = RET_QK, RET_V
    half = dk // 2

    @pl.when(pl.program_id(1) == 0)
    def _():
        st_ref[...] = jnp.zeros_like(st_ref)
        row = lax.broadcasted_iota(jnp.int32, (L, L), 0)
        col = lax.broadcasted_iota(jnp.int32, (L, L), 1)
        visible = (col // CHUNK) <= (row // CHUNK)
        dist = jnp.abs(row - col).astype(F32)
        pos = lax.broadcasted_iota(jnp.int32, (L, LANES), 0).astype(F32)
        for h in range(heads):
            dm_ref[h] = jnp.where(visible, jnp.exp(log_gamma[h] * dist), 0.0)
            qd_ref[h] = jnp.exp(log_gamma[h] * (pos + 1.0))
            kd_ref[h] = jnp.exp(log_gamma[h] * (L - 1.0 - pos))

    c = cos_ref[...]
    s = sin_ref[...]
    k_scale = dk ** -0.5
    for h in range(heads):
        q1 = p_ref[:, h * dk:h * dk + half].astype(F32)
        q2 = p_ref[:, h * dk + half:(h + 1) * dk].astype(F32)
        k1 = p_ref[:, (heads + h) * dk:(heads + h) * dk + half].astype(F32)
        k2 = p_ref[:, (heads + h) * dk + half:(heads + h + 1) * dk].astype(F32)
        v = p_ref[:, 2 * heads * dk + h * dv:2 * heads * dk + (h + 1) * dv]
        gate = p_ref[:, 2 * heads * dk + heads * dv + h * dv:2 * heads * dk + heads * dv + (h + 1) * dv]
        qr1, qr2 = q1 * c - q2 * s, q1 * s + q2 * c
        kr1, kr2 = (k1 * c - k2 * s) * k_scale, (k1 * s + k2 * c) * k_scale
        qd, kd = qd_ref[h], kd_ref[h]
        q_bf = jnp.concatenate([qr1, qr2], axis=-1).astype(BF16)
        k_bf = jnp.concatenate([kr1, kr2], axis=-1).astype(BF16)
        q_in = jnp.concatenate([qr1 * qd, qr2 * qd], axis=-1).astype(BF16)
        k_out = jnp.concatenate([kr1 * kd, kr2 * kd], axis=-1).astype(BF16)
        sc = lax.dot_general(q_bf, k_bf, (((1,), (1,)), ((), ())), preferred_element_type=F32) * dm_ref[h]
        y = (jnp.dot(sc.astype(BF16), v, preferred_element_type=F32)
             + jnp.dot(q_in, st_ref[h].astype(BF16), preferred_element_type=F32))
        upd = lax.dot_general(k_out, v, (((0,), (0,)), ((), ())), preferred_element_type=F32)
        st_ref[h] = st_ref[h] * math.exp(log_gamma[h] * L) + upd
        y = y * lax.rsqrt(jnp.mean(y * y, axis=-1, keepdims=True) + EPS) * ng_ref[:, h * dv:(h + 1) * dv]
        o_ref[:, h * dv:(h + 1) * dv] = (_silu(gate.astype(F32)) * y).astype(o_ref.dtype)


def _retention(proj, cos, sin, norm_g, bsz, heads, L=RET_L):
    t = proj.shape[0]
    steps = t // bsz // L
    log_gamma = tuple(math.log1p(-2.0 ** (-5.0 - h)) for h in range(heads))
    row = lambda w: pl.BlockSpec((L, w), lambda b, j: (b * steps + j, 0))
    return pl.pallas_call(
        functools.partial(_ret_kernel, heads=heads, log_gamma=log_gamma),
        grid=(bsz, steps),
        in_specs=[row(proj.shape[1]), row(LANES), row(LANES),
                  pl.BlockSpec((1, heads * RET_V), lambda b, j: (0, 0))],
        out_specs=row(heads * RET_V),
        out_shape=jax.ShapeDtypeStruct((t, heads * RET_V), BF16),
        scratch_shapes=[pltpu.VMEM((heads, RET_QK, RET_V), F32),
                        pltpu.VMEM((heads, L, L), F32),
                        pltpu.VMEM((heads, L, LANES), F32),
                        pltpu.VMEM((heads, L, LANES), F32)],
        compiler_params=_cparams(("parallel", "arbitrary"), 48),
        name="retention",
    )(proj, cos, sin, norm_g.astype(F32).reshape(1, heads * RET_V))


def _router_kernel(x_ref, g_ref, sh_ref, sc_ref, rw_ref, h_ref, idx_ref, gates_ref, *, n_experts):
    h = _norm_mod(x_ref[...], g_ref[...], sh_ref[0], sc_ref[0])
    h_ref[...] = h
    logits = jnp.dot(h, rw_ref[...], precision=lax.Precision.HIGHEST, preferred_element_type=F32)
    lane = lax.broadcasted_iota(jnp.int32, logits.shape, 1).astype(F32)
    lg = jnp.where(lane < n_experts, logits, -jnp.inf)
    m1 = jnp.max(lg, axis=-1, keepdims=True)
    i1 = jnp.min(jnp.where(lg == m1, lane, float(LANES)), axis=-1, keepdims=True)
    lg2 = jnp.where(lane == i1, -jnp.inf, lg)
    m2 = jnp.max(lg2, axis=-1, keepdims=True)
    i2 = jnp.min(jnp.where(lg2 == m2, lane, float(LANES)), axis=-1, keepdims=True)
    e2 = jnp.exp(m2 - m1)
    idx_ref[:, 0:1] = i1.astype(jnp.int32)
    idx_ref[:, 1:2] = i2.astype(jnp.int32)
    gates_ref[:, 0:1] = 1.0 / (1.0 + e2)
    gates_ref[:, 1:2] = e2 / (1.0 + e2)


def _router(x, g, shift, scale, router_w, tm=512):
    t, d = x.shape
    n_experts = router_w.shape[1]
    rw = jnp.zeros((d, LANES), F32).at[:, :n_experts].set(router_w.astype(F32))
    tiles_per_batch = t // shift.shape[0] // tm
    mod_spec = pl.BlockSpec((1, 1, d), lambda i: (i // tiles_per_batch, 0, 0))
    return pl.pallas_call(
        functools.partial(_router_kernel, n_experts=n_experts),
        grid=(t // tm,),
        in_specs=[pl.BlockSpec((tm, d), lambda i: (i, 0)),
                  pl.BlockSpec((1, d), lambda i: (0, 0)),
                  mod_spec, mod_spec,
                  pl.BlockSpec((d, LANES), lambda i: (0, 0))],
        out_specs=[pl.BlockSpec((tm, d), lambda i: (i, 0)),
                   pl.BlockSpec((tm, TOP_K), lambda i: (i, 0)),
                   pl.BlockSpec((tm, TOP_K), lambda i: (i, 0))],
        out_shape=[jax.ShapeDtypeStruct((t, d), F32),
                   jax.ShapeDtypeStruct((t, TOP_K), jnp.int32),
                   jax.ShapeDtypeStruct((t, TOP_K), F32)],
        compiler_params=_cparams(("parallel",), 48),
        name="router",
    )(x, g.reshape(1, d), shift, scale, rw)


def _row_copy(src_ref, dst_ref, src_row, dst_row, sem):
    return pltpu.make_async_copy(src_ref.at[pl.ds(src_row, 1)], dst_ref.at[pl.ds(dst_row, 1)], sem)


def _gather_kernel(idx_ref, src_ref, o_ref, sem):
    rows = o_ref.shape[0]

    def start(r, carry):
        _row_copy(src_ref, o_ref, idx_ref[0, 0, r], r, sem).start()
        return carry

    def wait(r, carry):
        _row_copy(src_ref, o_ref, 0, r, sem).wait()
        return carry

    lax.fori_loop(0, rows, start, 0, unroll=DMA_UNROLL)
    lax.fori_loop(0, rows, wait, 0, unroll=DMA_UNROLL)


def _gather_rows(src, idx, rows=MOE_ROWS):
    n = idx.shape[0]
    d = src.shape[1]
    return pl.pallas_call(
        _gather_kernel,
        grid=(n // rows,),
        in_specs=[pl.BlockSpec((1, 1, rows), lambda i: (i, 0, 0), memory_space=pltpu.SMEM),
                  pl.BlockSpec(memory_space=pl.ANY)],
        out_specs=pl.BlockSpec((rows, d), lambda i: (i, 0)),
        out_shape=jax.ShapeDtypeStruct((n, d), src.dtype),
        scratch_shapes=[pltpu.SemaphoreType.DMA(())],
        compiler_params=_cparams(("arbitrary",), 32),
        name="gather_rows",
    )(idx.reshape(n // rows, 1, rows), src)


def _expert_kernel(bexp_ref, bvalid_ref, xg_ref, wi_ref, wo_ref, o_ref, h_ref, acc_ref, *, tf):
    b = pl.program_id(0)

    @pl.when(bvalid_ref[b] != 0)
    def _():
        h_ref[...] = xg_ref[...].astype(BF16)
        _swiglu_accumulate(h_ref, wi_ref.at[0], wo_ref.at[0], acc_ref, tf)
        o_ref[...] = acc_ref[...]

    @pl.when(bvalid_ref[b] == 0)
    def _():
        o_ref[...] = jnp.zeros_like(o_ref)


def _expert_ffn(xg, block_exp, block_valid, wi, wo, rows=MOE_ROWS, tf=FFN_TF):
    cap, d = xg.shape
    hidden = wo.shape[1]
    row_block = pl.BlockSpec((rows, d), lambda b, be, bv: (b, 0))
    grid_spec = pltpu.PrefetchScalarGridSpec(
        num_scalar_prefetch=2,
        grid=(cap // rows,),
        in_specs=[row_block,
                  pl.BlockSpec((1, d, 2 * hidden), lambda b, be, bv: (be[b], 0, 0)),
                  pl.BlockSpec((1, hidden, d), lambda b, be, bv: (be[b], 0, 0),
                               pipeline_mode=pl.Buffered(1))],
        out_specs=row_block,
        scratch_shapes=[pltpu.VMEM((rows, d), BF16), pltpu.VMEM((rows, d), F32)],
    )
    return pl.pallas_call(
        functools.partial(_expert_kernel, tf=tf),
        grid_spec=grid_spec,
        out_shape=jax.ShapeDtypeStruct((cap, d), F32),
        compiler_params=_cparams(("arbitrary",), 56),
        name="expert_ffn",
    )(block_exp, block_valid, xg, wi, wo)


def _combine_kernel(d0_ref, d1_ref, gates_ref, x_ref, gate_ref, yb_ref, o_ref, buf0, buf1, sem):
    rows = o_ref.shape[0]

    def start(r, carry):
        _row_copy(yb_ref, buf0, d0_ref[0, 0, r], r, sem.at[0]).start()
        _row_copy(yb_ref, buf1, d1_ref[0, 0, r], r, sem.at[1]).start()
        return carry

    def wait(r, carry):
        _row_copy(yb_ref, buf0, 0, r, sem.at[0]).wait()
        _row_copy(yb_ref, buf1, 0, r, sem.at[1]).wait()
        return carry

    lax.fori_loop(0, rows, start, 0, unroll=DMA_UNROLL)
    lax.fori_loop(0, rows, wait, 0, unroll=DMA_UNROLL)
    gts = gates_ref[...]
    y = gts[:, 0:1] * buf0[...] + gts[:, 1:2] * buf1[...]
    o_ref[...] = x_ref[...] + gate_ref[0] * y


def _moe_combine(x, gate, yb, dest, gates, tm=256):
    t, d = x.shape
    tiles_per_batch = t // gate.shape[0] // tm
    slot = pl.BlockSpec((1, 1, tm), lambda i: (i, 0, 0), memory_space=pltpu.SMEM)
    return pl.pallas_call(
        _combine_kernel,
        grid=(t // tm,),
        in_specs=[slot, slot,
                  pl.BlockSpec((tm, TOP_K), lambda i: (i, 0)),
                  pl.BlockSpec((tm, d), lambda i: (i, 0)),
                  pl.BlockSpec((1, 1, d), lambda i: (i // tiles_per_batch, 0, 0)),
                  pl.BlockSpec(memory_space=pl.ANY)],
        out_specs=pl.BlockSpec((tm, d), lambda i: (i, 0)),
        out_shape=jax.ShapeDtypeStruct((t, d), F32),
        scratch_shapes=[pltpu.VMEM((tm, d), F32), pltpu.VMEM((tm, d), F32), pltpu.SemaphoreType.DMA((2,))],
        compiler_params=_cparams(("arbitrary",), 32),
        name="moe_combine",
    )(dest[:, 0].reshape(t // tm, 1, tm), dest[:, 1].reshape(t // tm, 1, tm), gates, x, gate, yb)


def _route(top_idx, n_experts, rows):
    t = top_idx.shape[0]
    n_assign = t * TOP_K
    e_flat = top_idx.reshape(-1)
    onehot = (e_flat[:, None] == jnp.arange(n_experts, dtype=jnp.int32)[None, :]).astype(jnp.int32)
    csum = jnp.cumsum(onehot, axis=0)
    rank = jnp.sum((csum - onehot) * onehot, axis=-1)
    counts = csum[-1]
    padded = (counts + rows - 1) // rows * rows
    pend = jnp.cumsum(padded)
    pstart = pend - padded
    dest = (pstart[e_flat] + rank).astype(jnp.int32)
    n_blocks = n_assign // rows + n_experts
    tok_flat = jnp.repeat(jnp.arange(t, dtype=jnp.int32), TOP_K)
    slot_tok = jnp.zeros((n_blocks * rows,), jnp.int32).at[dest].set(tok_flat)
    block_start = jnp.arange(n_blocks, dtype=jnp.int32) * rows
    block_exp = jnp.clip(jnp.searchsorted(pend, block_start, side="right"), 0, n_experts - 1).astype(jnp.int32)
    block_valid = (block_start < pend[-1]).astype(jnp.int32)
    return slot_tok, dest.reshape(t, TOP_K), block_exp, block_valid


def _moe(x, g, shift, scale, gate, router_w, w_in, w_out):
    n_experts = router_w.shape[1]
    h, top_idx, gates = _router(x, g, shift, scale, router_w)
    slot_tok, dest, block_exp, block_valid = _route(top_idx, n_experts, MOE_ROWS)
    xg = _gather_rows(h, slot_tok)
    yb = _expert_ffn(xg, block_exp, block_valid, w_in.astype(BF16), w_out.astype(BF16))
    return _moe_combine(x, gate, yb, dest, gates)


def _ssd_layer(x, g, shift, scale, gate, bsz, w_in, conv_w, conv_b, dt_bias, a_log, d_skip, norm_g, w_out):
    heads = dt_bias.shape[0]
    d_inner = heads * SSD_HEAD_DIM
    conv_dim = conv_w.shape[1]
    n_main = d_inner + conv_dim
    w_main = w_in[:, :n_main].astype(BF16)
    w_dt = jnp.zeros((w_in.shape[0], LANES), BF16).at[:, :heads].set(w_in[:, n_main:].astype(BF16))
    zx = _norm_mod_matmul(x, g, shift, scale, w_main, BF16, tn=1024)
    dt_raw = _norm_mod_matmul(x, g, shift, scale, w_dt, F32)
    y = _ssd_scan(zx, dt_raw, conv_w, conv_b, dt_bias, a_log, d_skip, norm_g, bsz)
    return _matmul_residual(y, w_out.astype(BF16), x, gate)


def _mla_layer(x, g, shift, scale, gate, bsz, mla_tables, w_in, q_norm_g, kv_norm_g, w_uq, w_ukv,
               q_nope_g, q_rope_g, k_nope_g, k_rope_g, w_out):
    lat_dim = MLA_Q_RANK + MLA_KV_RANK + LANES
    w_lat = jnp.zeros((w_in.shape[0], lat_dim), BF16).at[:, :w_in.shape[1]].set(w_in.astype(BF16))
    lat = _norm_mod_matmul(x, g, shift, scale, w_lat, F32)
    q, k, v = _mla_prep(lat, mla_tables, q_norm_g, kv_norm_g, w_uq, w_ukv, q_nope_g, q_rope_g, k_nope_g,
                        k_rope_g, bsz)
    o = _flash_attention(q, k, v)
    return _matmul_residual(o.reshape(x.shape[0], -1), w_out.astype(BF16), x, gate)


def _ret_layer(x, g, shift, scale, gate, bsz, ret_tables, w_in, norm_g, w_out):
    heads = w_out.shape[0] // RET_V
    proj = _norm_mod_matmul(x, g, shift, scale, w_in.astype(BF16), BF16, tn=1024)
    y = _retention(proj, ret_tables[0], ret_tables[1], norm_g, bsz, heads)
    return _matmul_residual(y, w_out.astype(BF16), x, gate)


def kernel(x, c, positions, ada_w, ada_b, norm1_g, norm2_g, ssd_w_in, ssd_conv_w, ssd_conv_b, ssd_dt_bias, ssd_a_log, ssd_d, ssd_norm_g, ssd_w_out, mla_w_in, mla_q_norm_g, mla_kv_norm_g, mla_w_uq, mla_w_ukv, mla_q_nope_g, mla_q_rope_g, mla_k_nope_g, mla_k_rope_g, mla_w_out, ret_w_in, ret_norm_g, ret_w_out, ffn_w_in, ffn_w_out, router_w, moe_w_in, moe_w_out):
    bsz, s, d = x.shape
    depth = ada_w.shape[0]
    mod = _ada_mod(c, ada_w, ada_b).reshape(depth, bsz, 6, 1, d)
    rc, rs, mc, ma, mb = _rope_tables(positions)
    xt = x.reshape(bsz * s, d)
    for i in range(depth):
        shift1, scale1, gate1, shift2, scale2, gate2 = (mod[i, :, m] for m in range(6))
        kind, j = i % N_MIXERS, i // N_MIXERS
        if kind == 0:
            xt = _ssd_layer(xt, norm1_g[i], shift1, scale1, gate1, bsz, ssd_w_in[j], ssd_conv_w[j], ssd_conv_b[j],
                            ssd_dt_bias[j], ssd_a_log[j], ssd_d[j], ssd_norm_g[j], ssd_w_out[j])
        elif kind == 1:
            xt = _mla_layer(xt, norm1_g[i], shift1, scale1, gate1, bsz, (mc, ma, mb), mla_w_in[j],
                            mla_q_norm_g[j], mla_kv_norm_g[j], mla_w_uq[j], mla_w_ukv[j], mla_q_nope_g[j],
                            mla_q_rope_g[j], mla_k_nope_g[j], mla_k_rope_g[j], mla_w_out[j])
        else:
            xt = _ret_layer(xt, norm1_g[i], shift1, scale1, gate1, bsz, (rc, rs), ret_w_in[j], ret_norm_g[j],
                            ret_w_out[j])
        if i % 2 == 0:
            xt = _ffn(xt, norm2_g[i], shift2, scale2, gate2, ffn_w_in[i // 2], ffn_w_out[i // 2])
        else:
            xt = _moe(xt, norm2_g[i], shift2, scale2, gate2, router_w[i // 2], moe_w_in[i // 2], moe_w_out[i // 2])
    return xt.reshape(bsz, s, d)
```

```python
import functools
import math

import numpy as np
import jax
import jax.numpy as jnp
from jax import lax
from jax.experimental import pallas as pl
from jax.experimental.pallas import tpu as pltpu

F32 = jnp.float32
BF16 = jnp.bfloat16
EPS = 1e-6
LANES = 128
MIB = 1024 * 1024

CHUNK = 64
SSD_HEAD_DIM = 64
SSD_GROUPS = 4
SSD_STATE = 128
SSD_CONV = 4
MLA_HEADS = 8
MLA_NOPE = 128
MLA_ROPE = 64
MLA_V = 128
MLA_Q_RANK = 256
MLA_KV_RANK = 256
ROPE_BASE = 10000.0
RET_QK = 256
RET_V = 512
N_MIXERS = 3
TOP_K = 2

SSD_L = 128
RET_L = 256
ATT_TK = 512
DMA_UNROLL = 16
MOE_ROWS = 512
FFN_TF = 256


def _cparams(sem, vmem_mib):
    return pltpu.CompilerParams(dimension_semantics=sem, vmem_limit_bytes=vmem_mib * MIB)


def _silu(v):
    return v * jax.nn.sigmoid(v)


def _norm_mod(x, g, shift, scale):
    y = x * lax.rsqrt(jnp.mean(x * x, axis=-1, keepdims=True) + EPS)
    return (y * g) * (1.0 + scale) + shift


def _ada_kernel(cb_ref, w_ref, b_ref, o_ref):
    nb = cb_ref.shape[0]
    tn = w_ref.shape[2]
    for jn in range(tn // LANES):
        cols = slice(jn * LANES, (jn + 1) * LANES)
        w = w_ref[0, :, cols]
        for b in range(nb):
            s = jnp.sum(w * cb_ref[b], axis=0, keepdims=True)
            o_ref[0, b:b + 1, cols] = s + b_ref[0, :, cols]


def _ada_mod(c, ada_w, ada_b, tn=1024):
    depth, d, n = ada_w.shape
    bsz = c.shape[0]
    cb = jnp.broadcast_to(c[:, :, None], (bsz, d, LANES))
    return pl.pallas_call(
        _ada_kernel,
        grid=(depth, n // tn),
        in_specs=[pl.BlockSpec((bsz, d, LANES), lambda l, j: (0, 0, 0)),
                  pl.BlockSpec((1, d, tn), lambda l, j: (l, 0, j)),
                  pl.BlockSpec((1, 1, tn), lambda l, j: (l, 0, j))],
        out_specs=pl.BlockSpec((1, bsz, tn), lambda l, j: (l, 0, j)),
        out_shape=jax.ShapeDtypeStruct((depth, bsz, n), F32),
        compiler_params=_cparams(("parallel", "parallel"), 32),
        name="ada_mod",
    )(cb, ada_w, ada_b.reshape(depth, 1, n))


def _nm_matmul_kernel(x_ref, g_ref, sh_ref, sc_ref, w_ref, o_ref, h_ref):
    @pl.when(pl.program_id(1) == 0)
    def _():
        h_ref[...] = _norm_mod(x_ref[...], g_ref[...], sh_ref[0], sc_ref[0]).astype(BF16)

    o_ref[...] = jnp.dot(h_ref[...], w_ref[...], preferred_element_type=F32).astype(o_ref.dtype)


def _norm_mod_matmul(x, g, shift, scale, w, out_dtype, tm=1024, tn=None):
    t, d = x.shape
    n = w.shape[1]
    tn = n if tn is None else tn
    tiles_per_batch = t // shift.shape[0] // tm
    mod_spec = pl.BlockSpec((1, 1, d), lambda i, j: (i // tiles_per_batch, 0, 0))
    return pl.pallas_call(
        _nm_matmul_kernel,
        grid=(t // tm, n // tn),
        in_specs=[pl.BlockSpec((tm, d), lambda i, j: (i, 0)),
                  pl.BlockSpec((1, d), lambda i, j: (0, 0)),
                  mod_spec, mod_spec,
                  pl.BlockSpec((d, tn), lambda i, j: (0, j))],
        out_specs=pl.BlockSpec((tm, tn), lambda i, j: (i, j)),
        out_shape=jax.ShapeDtypeStruct((t, n), out_dtype),
        scratch_shapes=[pltpu.VMEM((tm, d), BF16)],
        compiler_params=_cparams(("parallel", "arbitrary"), 48),
        name="norm_mod_matmul",
    )(x, g.reshape(1, d), shift, scale, w)


def _mm_res_kernel(y_ref, w_ref, x_ref, gate_ref, o_ref):
    o_ref[...] = x_ref[...] + gate_ref[0] * jnp.dot(y_ref[...], w_ref[...], preferred_element_type=F32)


def _matmul_residual(y, w, x, gate, tm=512):
    t, k = y.shape
    d = w.shape[1]
    tiles_per_batch = t // gate.shape[0] // tm
    return pl.pallas_call(
        _mm_res_kernel,
        grid=(t // tm,),
        in_specs=[pl.BlockSpec((tm, k), lambda i: (i, 0)),
                  pl.BlockSpec((k, d), lambda i: (0, 0)),
                  pl.BlockSpec((tm, d), lambda i: (i, 0)),
                  pl.BlockSpec((1, 1, d), lambda i: (i // tiles_per_batch, 0, 0))],
        out_specs=pl.BlockSpec((tm, d), lambda i: (i, 0)),
        out_shape=jax.ShapeDtypeStruct((t, d), F32),
        compiler_params=_cparams(("parallel",), 48),
        name="matmul_residual",
    )(y, w, x, gate)


def _swiglu_accumulate(h_ref, wi_ref, wo_ref, acc_ref, tf, per_chunk=None):
    hidden = wo_ref.shape[0]
    acc_ref[...] = jnp.zeros_like(acc_ref)

    def body(f, carry):
        if per_chunk is not None:
            per_chunk(f)
        lo = pl.multiple_of(f * tf, tf)
        h = h_ref[...]
        gt = jnp.dot(h, wi_ref[:, pl.ds(lo, tf)], preferred_element_type=F32)
        up = jnp.dot(h, wi_ref[:, pl.ds(hidden + lo, tf)], preferred_element_type=F32)
        act = (_silu(gt) * up).astype(BF16)
        acc_ref[...] += jnp.dot(act, wo_ref[pl.ds(lo, tf), :], preferred_element_type=F32)
        return carry

    lax.fori_loop(0, hidden // tf, body, 0)


def _ffn_kernel(x_ref, g_ref, sh_ref, sc_ref, gate_ref, wi_ref, wo_ref, o_ref, h_ref, acc_ref, *, tf):
    x = x_ref[...]
    h_ref[...] = _norm_mod(x, g_ref[...], sh_ref[0], sc_ref[0]).astype(BF16)
    _swiglu_accumulate(h_ref, wi_ref, wo_ref, acc_ref, tf)
    o_ref[...] = x + gate_ref[0] * acc_ref[...]


def _ffn(x, g, shift, scale, gate, w_in, w_out, tm=512, tf=FFN_TF):
    t, d = x.shape
    wi, wo = w_in.astype(BF16), w_out.astype(BF16)
    hidden = wo.shape[0]
    tiles_per_batch = t // gate.shape[0] // tm
    mod_spec = pl.BlockSpec((1, 1, d), lambda i: (i // tiles_per_batch, 0, 0))
    resident = pl.Buffered(1)
    return pl.pallas_call(
        functools.partial(_ffn_kernel, tf=tf),
        grid=(t // tm,),
        in_specs=[pl.BlockSpec((tm, d), lambda i: (i, 0)),
                  pl.BlockSpec((1, d), lambda i: (0, 0)),
                  mod_spec, mod_spec, mod_spec,
                  pl.BlockSpec((d, 2 * hidden), lambda i: (0, 0), pipeline_mode=resident),
                  pl.BlockSpec((hidden, d), lambda i: (0, 0), pipeline_mode=resident)],
        out_specs=pl.BlockSpec((tm, d), lambda i: (i, 0)),
        out_shape=jax.ShapeDtypeStruct((t, d), F32),
        scratch_shapes=[pltpu.VMEM((tm, d), BF16), pltpu.VMEM((tm, d), F32)],
        compiler_params=_cparams(("parallel",), 48),
        name="ffn",
    )(x, g.reshape(1, d), shift, scale, gate, wi, wo)


def _rope_kernel(pos_ref, inv_ret_ref, inv_mla_ref, rc_ref, rs_ref, mc_ref, ma_ref, mb_ref):
    pos = pos_ref[...]
    ang = pos * inv_ret_ref[...]
    rc_ref[...] = jnp.cos(ang)
    rs_ref[...] = jnp.sin(ang)
    half = MLA_ROPE // 2
    ang = pos * inv_mla_ref[...]
    lane = lax.broadcasted_iota(jnp.int32, ang.shape, 1)
    c = jnp.cos(ang)
    s = jnp.sin(ang)
    mc_ref[...] = jnp.where(lane < MLA_ROPE, c, 0.0)
    ma_ref[...] = jnp.where(lane < half, -s, 0.0)
    mb_ref[...] = jnp.where((lane >= half) & (lane < MLA_ROPE), s, 0.0)


def _rope_tables(positions, tm=1024):
    t = positions.size
    pos = positions.astype(F32).reshape(t, 1)
    inv_ret = ROPE_BASE ** (-jnp.arange(0, RET_QK, 2, dtype=F32) / RET_QK)
    inv_half = ROPE_BASE ** (-jnp.arange(0, MLA_ROPE, 2, dtype=F32) / MLA_ROPE)
    inv_mla = jnp.concatenate([inv_half, inv_half, jnp.zeros((LANES - MLA_ROPE,), F32)])
    row = pl.BlockSpec((tm, LANES), lambda i: (i, 0))
    const = pl.BlockSpec((1, LANES), lambda i: (0, 0))
    out = jax.ShapeDtypeStruct((t, LANES), F32)
    return pl.pallas_call(
        _rope_kernel,
        grid=(t // tm,),
        in_specs=[pl.BlockSpec((tm, 1), lambda i: (i, 0)), const, const],
        out_specs=[row] * 5,
        out_shape=[out] * 5,
        compiler_params=_cparams(("parallel",), 32),
        name="rope_tables",
    )(pos, inv_ret.reshape(1, LANES), inv_mla.reshape(1, LANES))


def _softplus(v):
    return jnp.maximum(v, 0.0) + jnp.log1p(jnp.exp(-jnp.abs(v)))


def _ssd_kernel(zx_ref, dt_ref, cw_ref, cb_ref, dtb_ref, ah_ref, dsk_ref, ng_ref, o_ref,
                ext_ref, act_ref, st_ref, acum_ref, acumT_ref, dtT_ref, wcol_ref, y_ref, xw_ref, dec_ref,
                *, d_inner, groups, n_state, head_dim):
    L = zx_ref.shape[0]
    conv_dim = act_ref.shape[1]
    heads_per_group = d_inner // groups // head_dim
    gw = d_inner // groups
    pairs = gw // LANES

    @pl.when(pl.program_id(1) == 0)
    def _():
        ext_ref[0:8, :] = jnp.zeros((8, conv_dim), F32)
        st_ref[...] = jnp.zeros_like(st_ref)

    ext_ref[8:8 + L, :] = zx_ref[:, d_inner:d_inner + conv_dim].astype(F32)
    conv = cb_ref[...]
    for k in range(SSD_CONV):
        conv = conv + cw_ref[k:k + 1, :] * ext_ref[5 + k:5 + k + L, :]
    act_ref[...] = _silu(conv)
    ext_ref[0:8, :] = ext_ref[L:L + 8, :]

    dt = _softplus(dt_ref[...] + dtb_ref[...])
    a = dt * ah_ref[...]
    row = lax.broadcasted_iota(jnp.int32, (L, L), 0)
    col = lax.broadcasted_iota(jnp.int32, (L, L), 1)
    tril = row >= col
    a_cum = jnp.dot(tril.astype(F32), a, precision=lax.Precision.HIGHEST, preferred_element_type=F32)
    a_last = a_cum[L - 1:L, :]
    acum_ref[...] = a_cum
    acumT_ref[...] = a_cum.T
    dtT_ref[...] = dt.T
    wcol_ref[...] = dt * jnp.exp(a_last - a_cum)

    lane = lax.broadcasted_iota(jnp.int32, (L, LANES), 1)
    lane_lo = lane < head_dim
    lane_lo_row = lane_lo[0:1, :]

    for g in range(groups):
        b_g = act_ref[:, d_inner + g * n_state:d_inner + (g + 1) * n_state].astype(BF16)
        c_g = act_ref[:, d_inner + (groups + g) * n_state:d_inner + (groups + g + 1) * n_state].astype(BF16)
        cb = lax.dot_general(c_g, b_g, (((1,), (1,)), ((), ())), preferred_element_type=F32)
        y_off = jnp.dot(c_g, st_ref[g].astype(BF16), preferred_element_type=F32)
        for p in range(pairs):
            cols = slice(g * gw + p * LANES, g * gw + (p + 1) * LANES)
            xs_pair = act_ref[:, cols]
            xs_bf = xs_pair.astype(BF16)
            res, e_l, w_l = [], [], []
            for hh in range(LANES // head_dim):
                h = g * heads_per_group + p * (LANES // head_dim) + hh
                a_l = jnp.broadcast_to(acum_ref[:, h:h + 1], (L, L))
                m = jnp.where(tril, jnp.exp(a_l - acumT_ref[h:h + 1, :]), 0.0) * cb * dtT_ref[h:h + 1, :]
                res.append(jnp.dot(m.astype(BF16), xs_bf, preferred_element_type=F32))
                e_l.append(jnp.exp(jnp.broadcast_to(acum_ref[:, h:h + 1], (L, LANES))))
                w_l.append(jnp.broadcast_to(wcol_ref[:, h:h + 1], (L, LANES)))
            yo = y_off[:, p * LANES:(p + 1) * LANES]
            y_ref[:, cols] = (jnp.where(lane_lo, res[0] + e_l[0] * yo, res[1] + e_l[1] * yo)
                              + xs_pair * dsk_ref[:, cols])
            xw_ref[:, p * LANES:(p + 1) * LANES] = (xs_pair * jnp.where(lane_lo, w_l[0], w_l[1])).astype(BF16)
            dec_ref[:, p * LANES:(p + 1) * LANES] = jnp.where(lane_lo_row, e_l[0][L - 1:L, :], e_l[1][L - 1:L, :])
        upd = lax.dot_general(b_g, xw_ref[...], (((0,), (0,)), ((), ())), preferred_element_type=F32)
        st_ref[g] = st_ref[g] * dec_ref[...] + upd

    for g in range(groups):
        cols = slice(g * gw, (g + 1) * gw)
        yg = y_ref[:, cols] * _silu(zx_ref[:, cols].astype(F32))
        yg = yg * lax.rsqrt(jnp.mean(yg * yg, axis=-1, keepdims=True) + EPS)
        o_ref[:, cols] = (yg * ng_ref[:, cols]).astype(o_ref.dtype)


def _ssd_scan(zx, dt_raw, conv_w, conv_b, dt_bias, a_log, d_skip, norm_g, bsz, L=SSD_L):
    t = zx.shape[0]
    s = t // bsz
    heads = dt_bias.shape[0]
    d_inner = heads * SSD_HEAD_DIM
    conv_dim = conv_w.shape[1]
    gw = d_inner // SSD_GROUPS
    steps = s // L

    def pad_heads(v):
        return jnp.zeros((1, LANES), F32).at[0, :heads].set(v.astype(F32))

    a_head = pad_heads(-jnp.exp(a_log.astype(F32)))
    d_exp = jnp.repeat(d_skip.astype(F32), SSD_HEAD_DIM).reshape(1, d_inner)
    row = lambda w: pl.BlockSpec((L, w), lambda b, j: (b * steps + j, 0))
    const = lambda r, w: pl.BlockSpec((r, w), lambda b, j: (0, 0))
    kern = functools.partial(_ssd_kernel, d_inner=d_inner, groups=SSD_GROUPS, n_state=SSD_STATE,
                             head_dim=SSD_HEAD_DIM)
    return pl.pallas_call(
        kern,
        grid=(bsz, steps),
        in_specs=[row(zx.shape[1]), row(LANES), const(SSD_CONV, conv_dim), const(1, conv_dim),
                  const(1, LANES), const(1, LANES), const(1, d_inner), const(1, d_inner)],
        out_specs=row(d_inner),
        out_shape=jax.ShapeDtypeStruct((t, d_inner), BF16),
        scratch_shapes=[pltpu.VMEM((L + 8, conv_dim), F32),
                        pltpu.VMEM((L, conv_dim), F32),
                        pltpu.VMEM((SSD_GROUPS, SSD_STATE, gw), F32),
                        pltpu.VMEM((L, LANES), F32),
                        pltpu.VMEM((LANES, L), F32),
                        pltpu.VMEM((LANES, L), F32),
                        pltpu.VMEM((L, LANES), F32),
                        pltpu.VMEM((L, d_inner), F32),
                        pltpu.VMEM((L, gw), BF16),
                        pltpu.VMEM((1, gw), F32)],
        compiler_params=_cparams(("parallel", "arbitrary"), 48),
        name="ssd_scan",
    )(zx, dt_raw, conv_w.astype(F32), conv_b.astype(F32).reshape(1, conv_dim), pad_heads(dt_bias), a_head,
      d_exp, norm_g.astype(F32).reshape(1, d_inner))


def _rms_rows(v, n):
    return v * lax.rsqrt(jnp.sum(v * v, axis=-1, keepdims=True) / n + EPS)


def _mla_prep_kernel(lat_ref, qg_ref, kvg_ref, wqn_ref, wqp_ref, wkv_ref, gqn_ref, gqp_ref, gkn_ref, gkp_ref,
                     mc_ref, ma_ref, mb_ref, q_ref, k_ref, v_ref, *, scale):
    lat = lat_ref[...]
    cq = lat[:, :MLA_Q_RANK]
    ckv = lat[:, MLA_Q_RANK:MLA_Q_RANK + MLA_KV_RANK]
    kpe = lat[:, MLA_Q_RANK + MLA_KV_RANK:]
    qn = (_rms_rows(cq, MLA_Q_RANK) * qg_ref[...]).astype(BF16)
    kvn = (_rms_rows(ckv, MLA_KV_RANK) * kvg_ref[...]).astype(BF16)
    q_nope = jnp.dot(qn, wqn_ref[...], preferred_element_type=F32)
    q_pe = jnp.dot(qn, wqp_ref[...], preferred_element_type=F32)
    kv = jnp.dot(kvn, wkv_ref[...], preferred_element_type=F32)
    mc, ma, mb = mc_ref[...], ma_ref[...], mb_ref[...]

    def rotate(v):
        return v * mc + pltpu.roll(v, LANES - MLA_ROPE // 2, 1) * ma + pltpu.roll(v, MLA_ROPE // 2, 1) * mb

    k_pe = rotate(_rms_rows(kpe, MLA_ROPE) * gkp_ref[...]).astype(BF16)
    for h in range(MLA_HEADS):
        qn_h = _rms_rows(q_nope[:, h * LANES:(h + 1) * LANES], MLA_NOPE) * gqn_ref[...]
        qp_h = rotate(_rms_rows(q_pe[:, h * LANES:(h + 1) * LANES], MLA_ROPE) * gqp_ref[...])
        q_ref[0, h, :, 0:LANES] = (qn_h * scale).astype(BF16)
        q_ref[0, h, :, LANES:2 * LANES] = (qp_h * scale).astype(BF16)
        kn_h = _rms_rows(kv[:, 2 * h * LANES:(2 * h + 1) * LANES], MLA_NOPE) * gkn_ref[...]
        k_ref[0, h, :, 0:LANES] = kn_h.astype(BF16)
        k_ref[0, h, :, LANES:2 * LANES] = k_pe
        v_ref[0, h] = kv[:, (2 * h + 1) * LANES:(2 * h + 2) * LANES].astype(BF16)


def _pad_lanes(v, n=LANES):
    return jnp.zeros((1, n), F32).at[0, :v.shape[0]].set(v.astype(F32))


def _mla_prep(lat, tables, q_norm_g, kv_norm_g, w_uq, w_ukv, q_nope_g, q_rope_g, k_nope_g, k_rope_g, bsz, tm=256):
    t = lat.shape[0]
    s = t // bsz
    steps = s // tm
    hh = MLA_HEADS
    w_uq = w_uq.reshape(MLA_Q_RANK, hh, MLA_NOPE + MLA_ROPE)
    wqn = w_uq[:, :, :MLA_NOPE].reshape(MLA_Q_RANK, hh * MLA_NOPE).astype(BF16)
    wqp = jnp.pad(w_uq[:, :, MLA_NOPE:], ((0, 0), (0, 0), (0, LANES - MLA_ROPE)))
    wqp = wqp.reshape(MLA_Q_RANK, hh * LANES).astype(BF16)
    wkv = w_ukv.astype(BF16)
    mc, ma, mb = tables
    row = pl.BlockSpec((tm, LANES), lambda b, i: (b * steps + i, 0))
    const = lambda r, w: pl.BlockSpec((r, w), lambda b, i: (0, 0))
    head_out = lambda w: pl.BlockSpec((1, hh, tm, w), lambda b, i: (b, 0, i, 0))
    return pl.pallas_call(
        functools.partial(_mla_prep_kernel, scale=(MLA_NOPE + MLA_ROPE) ** -0.5),
        grid=(bsz, steps),
        in_specs=[pl.BlockSpec((tm, lat.shape[1]), lambda b, i: (b * steps + i, 0)),
                  const(1, MLA_Q_RANK), const(1, MLA_KV_RANK),
                  const(MLA_Q_RANK, hh * LANES), const(MLA_Q_RANK, hh * LANES), const(MLA_KV_RANK, hh * 2 * LANES),
                  const(1, LANES), const(1, LANES), const(1, LANES), const(1, LANES),
                  row, row, row],
        out_specs=[head_out(2 * LANES), head_out(2 * LANES), head_out(LANES)],
        out_shape=[jax.ShapeDtypeStruct((bsz, hh, s, 2 * LANES), BF16),
                   jax.ShapeDtypeStruct((bsz, hh, s, 2 * LANES), BF16),
                   jax.ShapeDtypeStruct((bsz, hh, s, LANES), BF16)],
        compiler_params=_cparams(("parallel", "parallel"), 48),
        name="mla_prep",
    )(lat, _pad_lanes(q_norm_g, MLA_Q_RANK), _pad_lanes(kv_norm_g, MLA_KV_RANK), wqn, wqp, wkv,
      _pad_lanes(q_nope_g), _pad_lanes(q_rope_g), _pad_lanes(k_nope_g), _pad_lanes(k_rope_g), mc, ma, mb)


def _flash_kernel(q_ref, k_ref, v_ref, o_ref, s_ref, m_ref, l_ref, acc_ref, *, tq, tk):
    i = pl.program_id(2)
    n_strips = tq // CHUNK
    chunks_per_tile = tk // CHUNK

    def scores(j, slot):
        kt = k_ref[0, 0, pl.ds(pl.multiple_of(j * tk, tk), tk), :]
        s_ref[slot] = lax.dot_general(q_ref[0, 0], kt, (((1,), (1,)), ((), ())), preferred_element_type=F32)

    def update(j, slot, diag=None):
        lane = lax.broadcasted_iota(jnp.int32, (CHUNK, LANES), 1)
        m_all, l_all = m_ref[...], l_ref[...]
        m_out, l_out, a_out, p_out = [], [], [], []
        for r in range(n_strips):
            rows = slice(r * CHUNK, (r + 1) * CHUNK)
            m_prev, l_prev = m_all[rows], l_all[rows]
            visible = tk if diag is None else min(max((r - diag * chunks_per_tile + 1) * CHUNK, 0), tk)
            n_blocks = -(-visible // LANES)
            hidden_cols = [jnp.zeros((CHUNK, tk - n_blocks * LANES), BF16)] if n_blocks < tk // LANES else []
            if visible == 0:
                m_out.append(m_prev)
                l_out.append(l_prev)
                a_out.append(jnp.ones((CHUNK, LANES), F32))
                p_out.append(hidden_cols[0])
                continue
            blocks = [s_ref[slot, rows, c * LANES:(c + 1) * LANES] for c in range(n_blocks)]
            if visible % LANES:
                blocks[-1] = jnp.where(lane < visible % LANES, blocks[-1], -jnp.inf)
            row_max = jnp.max(functools.reduce(jnp.maximum, blocks), axis=-1, keepdims=True)
            m_new = jnp.maximum(m_prev, jnp.broadcast_to(row_max, (CHUNK, LANES)))
            alpha = jnp.exp(m_prev - m_new)
            ps = [jnp.exp(b - m_new) for b in blocks]
            p_out.append(jnp.concatenate([p.astype(BF16) for p in ps] + hidden_cols, axis=1))
            m_out.append(m_new)
            l_out.append(alpha * l_prev + functools.reduce(jnp.add, ps))
            a_out.append(alpha)
        m_ref[...] = jnp.concatenate(m_out, axis=0)
        l_ref[...] = jnp.concatenate(l_out, axis=0)
        vt = v_ref[0, 0, pl.ds(pl.multiple_of(j * tk, tk), tk), :]
        acc_ref[...] = (jnp.concatenate(a_out, axis=0) * acc_ref[...]
                        + jnp.dot(jnp.concatenate(p_out, axis=0), vt, preferred_element_type=F32))

    m_ref[...] = jnp.full_like(m_ref, -jnp.inf)
    l_ref[...] = jnp.zeros_like(l_ref)
    acc_ref[...] = jnp.zeros_like(acc_ref)
    scores(0, 0)

    def body(t, carry):
        scores(2 * t + 1, 1)
        update(2 * t, 0)
        scores(2 * t + 2, 0)
        update(2 * t + 1, 1)
        return carry

    lax.fori_loop(0, i, body, 0)
    scores(2 * i + 1, 1)
    update(2 * i, 0, diag=0)
    update(2 * i + 1, 1, diag=1)
    o_ref[0] = (acc_ref[...] / jnp.sum(l_ref[...], axis=-1, keepdims=True)).astype(o_ref.dtype)


def _flash_attention(q, k, v, tk=ATT_TK):
    bsz, hh, s, dq = q.shape
    dv = v.shape[-1]
    tq = 2 * tk
    return pl.pallas_call(
        functools.partial(_flash_kernel, tq=tq, tk=tk),
        grid=(bsz, hh, s // tq),
        in_specs=[pl.BlockSpec((1, 1, tq, dq), lambda b, h, i: (b, h, i, 0)),
                  pl.BlockSpec((1, 1, s, dq), lambda b, h, i: (b, h, 0, 0)),
                  pl.BlockSpec((1, 1, s, dv), lambda b, h, i: (b, h, 0, 0))],
        out_specs=pl.BlockSpec((1, tq, dv), lambda b, h, i: (b, i, h)),
        out_shape=jax.ShapeDtypeStruct((bsz, s, hh * dv), BF16),
        scratch_shapes=[pltpu.VMEM((2, tq, tk), F32),
                        pltpu.VMEM((tq, LANES), F32),
                        pltpu.VMEM((tq, LANES), F32),
                        pltpu.VMEM((tq, dv), F32)],
        compiler_params=_cparams(("parallel", "parallel", "arbitrary"), 48),
        name="flash_attention",
    )(q, k, v)


def _ret_kernel(p_ref, cos_ref, sin_ref, ng_ref, o_ref, st_ref, dm_ref, qd_ref, kd_ref, *, heads, log_gamma):
    L = p_ref.shape[0]
    dk, dv = RET_QK, RET_V
    half = dk // 2

    @pl.when(pl.program_id(1) == 0)
    def _():
        st_ref[...] = jnp.zeros_like(st_ref)
        row = lax.broadcasted_iota(jnp.int32, (L, L), 0)
        col = lax.broadcasted_iota(jnp.int32, (L, L), 1)
        visible = (col // CHUNK) <= (row // CHUNK)
        dist = jnp.abs(row - col).astype(F32)
        pos = lax.broadcasted_iota(jnp.int32, (L, LANES), 0).astype(F32)
        for h in range(heads):
            dm_ref[h] = jnp.where(visible, jnp.exp(log_gamma[h] * dist), 0.0)
            qd_ref[h] = jnp.exp(log_gamma[h] * (pos + 1.0))
            kd_ref[h] = jnp.exp(log_gamma[h] * (L - 1.0 - pos))

    c = cos_ref[...]
    s = sin_ref[...]
    k_scale = dk ** -0.5
    for h in range(heads):
        q1 = p_ref[:, h * dk:h * dk + half].astype(F32)
        q2 = p_ref[:, h * dk + half:(h + 1) * dk].astype(F32)
        k1 = p_ref[:, (heads + h) * dk:(heads + h) * dk + half].astype(F32)
        k2 = p_ref[:, (heads + h) * dk + half:(heads + h + 1) * dk].astype(F32)
        v = p_ref[:, 2 * heads * dk + h * dv:2 * heads * dk + (h + 1) * dv]
        gate = p_ref[:, 2 * heads * dk + heads * dv + h * dv:2 * heads * dk + heads * dv + (h + 1) * dv]
        qr1, qr2 = q1 * c - q2 * s, q1 * s + q2 * c
        kr1, kr2 = (k1 * c - k2 * s) * k_scale, (k1 * s + k2 * c) * k_scale
        qd, kd = qd_ref[h], kd_ref[h]
        q_bf = jnp.concatenate([qr1, qr2], axis=-1).astype(BF16)
        k_bf = jnp.concatenate([kr1, kr2], axis=-1).astype(BF16)
        q_in = jnp.concatenate([qr1 * qd, qr2 * qd], axis=-1).astype(BF16)
        k_out = jnp.concatenate([kr1 * kd, kr2 * kd], axis=-1).astype(BF16)
        sc = lax.dot_general(q_bf, k_bf, (((1,), (1,)), ((), ())), preferred_element_type=F32) * dm_ref[h]
        y = (jnp.dot(sc.astype(BF16), v, preferred_element_type=F32)
             + jnp.dot(q_in, st_ref[h].astype(BF16), preferred_element_type=F32))
        upd = lax.dot_general(k_out, v, (((0,), (0,)), ((), ())), preferred_element_type=F32)
        st_ref[h] = st_ref[h] * math.exp(log_gamma[h] * L) + upd
        y = y * lax.rsqrt(jnp.mean(y * y, axis=-1, keepdims=True) + EPS) * ng_ref[:, h * dv:(h + 1) * dv]
        o_ref[:, h * dv:(h + 1) * dv] = (_silu(gate.astype(F32)) * y).astype(o_ref.dtype)


def _retention(proj, cos, sin, norm_g, bsz, heads, L=RET_L):
    t = proj.shape[0]
    steps = t // bsz // L
    log_gamma = tuple(math.log1p(-2.0 ** (-5.0 - h)) for h in range(heads))
    row = lambda w: pl.BlockSpec((L, w), lambda b, j: (b * steps + j, 0))
    return pl.pallas_call(
        functools.partial(_ret_kernel, heads=heads, log_gamma=log_gamma),
        grid=(bsz, steps),
        in_specs=[row(proj.shape[1]), row(LANES), row(LANES),
                  pl.BlockSpec((1, heads * RET_V), lambda b, j: (0, 0))],
        out_specs=row(heads * RET_V),
        out_shape=jax.ShapeDtypeStruct((t, heads * RET_V), BF16),
        scratch_shapes=[pltpu.VMEM((heads, RET_QK, RET_V), F32),
                        pltpu.VMEM((heads, L, L), F32),
                        pltpu.VMEM((heads, L, LANES), F32),
                        pltpu.VMEM((heads, L, LANES), F32)],
        compiler_params=_cparams(("parallel", "arbitrary"), 48),
        name="retention",
    )(proj, cos, sin, norm_g.astype(F32).reshape(1, heads * RET_V))


def _router_kernel(x_ref, g_ref, sh_ref, sc_ref, rw_ref, h_ref, idx_ref, gates_ref, *, n_experts):
    h = _norm_mod(x_ref[...], g_ref[...], sh_ref[0], sc_ref[0])
    h_ref[...] = h
    logits = jnp.dot(h, rw_ref[...], precision=lax.Precision.HIGHEST, preferred_element_type=F32)
    lane = lax.broadcasted_iota(jnp.int32, logits.shape, 1).astype(F32)
    lg = jnp.where(lane < n_experts, logits, -jnp.inf)
    m1 = jnp.max(lg, axis=-1, keepdims=True)
    i1 = jnp.min(jnp.where(lg == m1, lane, float(LANES)), axis=-1, keepdims=True)
    lg2 = jnp.where(lane == i1, -jnp.inf, lg)
    m2 = jnp.max(lg2, axis=-1, keepdims=True)
    i2 = jnp.min(jnp.where(lg2 == m2, lane, float(LANES)), axis=-1, keepdims=True)
    e2 = jnp.exp(m2 - m1)
    idx_ref[:, 0:1] = i1.astype(jnp.int32)
    idx_ref[:, 1:2] = i2.astype(jnp.int32)
    gates_ref[:, 0:1] = 1.0 / (1.0 + e2)
    gates_ref[:, 1:2] = e2 / (1.0 + e2)


def _router(x, g, shift, scale, router_w, tm=512):
    t, d = x.shape
    n_experts = router_w.shape[1]
    rw = jnp.zeros((d, LANES), F32).at[:, :n_experts].set(router_w.astype(F32))
    tiles_per_batch = t // shift.shape[0] // tm
    mod_spec = pl.BlockSpec((1, 1, d), lambda i: (i // tiles_per_batch, 0, 0))
    return pl.pallas_call(
        functools.partial(_router_kernel, n_experts=n_experts),
        grid=(t // tm,),
        in_specs=[pl.BlockSpec((tm, d), lambda i: (i, 0)),
                  pl.BlockSpec((1, d), lambda i: (0, 0)),
                  mod_spec, mod_spec,
                  pl.BlockSpec((d, LANES), lambda i: (0, 0))],
        out_specs=[pl.BlockSpec((tm, d), lambda i: (i, 0)),
                   pl.BlockSpec((tm, TOP_K), lambda i: (i, 0)),
                   pl.BlockSpec((tm, TOP_K), lambda i: (i, 0))],
        out_shape=[jax.ShapeDtypeStruct((t, d), F32),
                   jax.ShapeDtypeStruct((t, TOP_K), jnp.int32),
                   jax.ShapeDtypeStruct((t, TOP_K), F32)],
        compiler_params=_cparams(("parallel",), 48),
        name="router",
    )(x, g.reshape(1, d), shift, scale, rw)


def _row_copy(src_ref, dst_ref, src_row, dst_row, sem):
    return pltpu.make_async_copy(src_ref.at[pl.ds(src_row, 1)], dst_ref.at[pl.ds(dst_row, 1)], sem)


def _start_row_copies(tok_ref, src_ref, dst_ref, sem, first, count):
    for u in range(count):
        r = first + u
        _row_copy(src_ref, dst_ref, tok_ref[0, 0, r], r, sem).start(priority=u % 2)


def _wait_row_copies(src_ref, dst_ref, sem):
    def wait(r, carry):
        _row_copy(src_ref, dst_ref, 0, r, sem).wait()
        return carry

    lax.fori_loop(0, dst_ref.shape[0], wait, 0, unroll=DMA_UNROLL)


def _expert_kernel(bexp_ref, bvalid_ref, tok_first_ref, tok_next_ref, h_hbm, wi_ref, wo_ref, o_ref,
                   xbuf, h_ref, acc_ref, sem, *, tf):
    b = pl.program_id(0)
    last = pl.num_programs(0) - 1
    slot = b % 2
    valid = bvalid_ref[b] != 0
    rows = h_ref.shape[0]
    n_chunks = wo_ref.shape[1] // tf
    per = rows // n_chunks
    next_buf, next_sem = xbuf.at[1 - slot], sem.at[1 - slot]

    def start_all(tok_ref, buf, buf_sem):
        def start(r, carry):
            _start_row_copies(tok_ref, h_hbm, buf, buf_sem, r * DMA_UNROLL, DMA_UNROLL)
            return carry

        lax.fori_loop(0, rows // DMA_UNROLL, start, 0)

    @pl.when(b == 0)
    def _():
        start_all(tok_first_ref, xbuf.at[0], sem.at[0])

    _wait_row_copies(h_hbm, xbuf.at[slot], sem.at[slot])

    @pl.when(valid)
    def _():
        h_ref[...] = xbuf[slot].astype(BF16)
        _start_row_copies(tok_next_ref, h_hbm, next_buf, next_sem, per * n_chunks, rows - per * n_chunks)
        _swiglu_accumulate(h_ref, wi_ref.at[0], wo_ref.at[0], acc_ref, tf,
                           per_chunk=lambda f: _start_row_copies(tok_next_ref, h_hbm, next_buf, next_sem,
                                                                 per * f, per))
        o_ref[...] = acc_ref[...]

    @pl.when(jnp.logical_not(valid))
    def _():
        start_all(tok_next_ref, next_buf, next_sem)
        o_ref[...] = jnp.zeros_like(o_ref)

    @pl.when(b == last)
    def _():
        _wait_row_copies(h_hbm, next_buf, next_sem)


def _expert_ffn(h, slot_tok, block_exp, block_valid, wi, wo, rows=MOE_ROWS, tf=FFN_TF):
    d = h.shape[1]
    hidden = wo.shape[1]
    n_blocks = slot_tok.shape[0] // rows
    tok_spec = lambda index: pl.BlockSpec((1, 1, rows), index, memory_space=pltpu.SMEM)
    grid_spec = pltpu.PrefetchScalarGridSpec(
        num_scalar_prefetch=2,
        grid=(n_blocks,),
        in_specs=[tok_spec(lambda b, be, bv: (0, 0, 0)),
                  tok_spec(lambda b, be, bv: (jnp.minimum(b + 1, n_blocks - 1), 0, 0)),
                  pl.BlockSpec(memory_space=pl.ANY),
                  pl.BlockSpec((1, d, 2 * hidden), lambda b, be, bv: (be[b], 0, 0)),
                  pl.BlockSpec((1, hidden, d), lambda b, be, bv: (be[b], 0, 0),
                               pipeline_mode=pl.Buffered(1))],
        out_specs=pl.BlockSpec((rows, d), lambda b, be, bv: (b, 0)),
        scratch_shapes=[pltpu.VMEM((2, rows, d), F32),
                        pltpu.VMEM((rows, d), BF16), pltpu.VMEM((rows, d), F32),
                        pltpu.SemaphoreType.DMA((2,))],
    )
    tok = slot_tok.reshape(n_blocks, 1, rows)
    return pl.pallas_call(
        functools.partial(_expert_kernel, tf=tf),
        grid_spec=grid_spec,
        out_shape=jax.ShapeDtypeStruct((n_blocks * rows, d), F32),
        compiler_params=_cparams(("arbitrary",), 56),
        name="expert_ffn",
    )(block_exp, block_valid, tok, tok, h, wi, wo)


def _combine_kernel(d0_ref, d1_ref, gates_ref, x_ref, gate_ref, yb_ref, o_ref, buf0, buf1, sem):
    rows = o_ref.shape[0]

    def start(r, carry):
        _row_copy(yb_ref, buf0, d0_ref[0, 0, r], r, sem.at[0]).start(priority=0)
        _row_copy(yb_ref, buf1, d1_ref[0, 0, r], r, sem.at[1]).start(priority=1)
        return carry

    def wait(r, carry):
        _row_copy(yb_ref, buf0, 0, r, sem.at[0]).wait()
        _row_copy(yb_ref, buf1, 0, r, sem.at[1]).wait()
        return carry

    lax.fori_loop(0, rows, start, 0, unroll=DMA_UNROLL)
    lax.fori_loop(0, rows, wait, 0, unroll=DMA_UNROLL)
    gts = gates_ref[...]
    y = gts[:, 0:1] * buf0[...] + gts[:, 1:2] * buf1[...]
    o_ref[...] = x_ref[...] + gate_ref[0] * y


def _moe_combine(x, gate, yb, dest, gates, tm=256):
    t, d = x.shape
    tiles_per_batch = t // gate.shape[0] // tm
    slot = pl.BlockSpec((1, 1, tm), lambda i: (i, 0, 0), memory_space=pltpu.SMEM)
    return pl.pallas_call(
        _combine_kernel,
        grid=(t // tm,),
        in_specs=[slot, slot,
                  pl.BlockSpec((tm, TOP_K), lambda i: (i, 0)),
                  pl.BlockSpec((tm, d), lambda i: (i, 0)),
                  pl.BlockSpec((1, 1, d), lambda i: (i // tiles_per_batch, 0, 0)),
                  pl.BlockSpec(memory_space=pl.ANY)],
        out_specs=pl.BlockSpec((tm, d), lambda i: (i, 0)),
        out_shape=jax.ShapeDtypeStruct((t, d), F32),
        scratch_shapes=[pltpu.VMEM((tm, d), F32), pltpu.VMEM((tm, d), F32), pltpu.SemaphoreType.DMA((2,))],
        compiler_params=_cparams(("arbitrary",), 32),
        name="moe_combine",
    )(dest[:, 0].reshape(t // tm, 1, tm), dest[:, 1].reshape(t // tm, 1, tm), gates, x, gate, yb)


def _route(top_idx, n_experts, rows):
    t = top_idx.shape[0]
    n_assign = t * TOP_K
    e_flat = top_idx.reshape(-1)
    onehot = (e_flat[:, None] == jnp.arange(n_experts, dtype=jnp.int32)[None, :]).astype(jnp.int32)
    csum = jnp.cumsum(onehot, axis=0)
    rank = jnp.sum((csum - onehot) * onehot, axis=-1)
    counts = csum[-1]
    padded = (counts + rows - 1) // rows * rows
    pend = jnp.cumsum(padded)
    pstart = pend - padded
    dest = (pstart[e_flat] + rank).astype(jnp.int32)
    n_blocks = n_assign // rows + n_experts
    tok_flat = jnp.repeat(jnp.arange(t, dtype=jnp.int32), TOP_K)
    slot_tok = jnp.zeros((n_blocks * rows,), jnp.int32).at[dest].set(tok_flat)
    block_start = jnp.arange(n_blocks, dtype=jnp.int32) * rows
    block_exp = jnp.clip(jnp.searchsorted(pend, block_start, side="right"), 0, n_experts - 1).astype(jnp.int32)
    block_valid = (block_start < pend[-1]).astype(jnp.int32)
    return slot_tok, dest.reshape(t, TOP_K), block_exp, block_valid


def _moe(x, g, shift, scale, gate, router_w, wi_all, wo_all, layer):
    n_experts = router_w.shape[1]
    h, top_idx, gates = _router(x, g, shift, scale, router_w)
    slot_tok, dest, block_exp, block_valid = _route(top_idx, n_experts, MOE_ROWS)
    yb = _expert_ffn(h, slot_tok, block_exp + layer * n_experts, block_valid, wi_all, wo_all)
    return _moe_combine(x, gate, yb, dest, gates)


def _ssd_layer(x, g, shift, scale, gate, bsz, w_in, conv_w, conv_b, dt_bias, a_log, d_skip, norm_g, w_out):
    heads = dt_bias.shape[0]
    d_inner = heads * SSD_HEAD_DIM
    conv_dim = conv_w.shape[1]
    n_main = d_inner + conv_dim
    w_main = w_in[:, :n_main].astype(BF16)
    w_dt = jnp.zeros((w_in.shape[0], LANES), BF16).at[:, :heads].set(w_in[:, n_main:].astype(BF16))
    zx = _norm_mod_matmul(x, g, shift, scale, w_main, BF16, tn=1024)
    dt_raw = _norm_mod_matmul(x, g, shift, scale, w_dt, F32)
    y = _ssd_scan(zx, dt_raw, conv_w, conv_b, dt_bias, a_log, d_skip, norm_g, bsz)
    return _matmul_residual(y, w_out.astype(BF16), x, gate)


def _mla_layer(x, g, shift, scale, gate, bsz, mla_tables, w_in, q_norm_g, kv_norm_g, w_uq, w_ukv,
               q_nope_g, q_rope_g, k_nope_g, k_rope_g, w_out):
    lat_dim = MLA_Q_RANK + MLA_KV_RANK + LANES
    w_lat = jnp.zeros((w_in.shape[0], lat_dim), BF16).at[:, :w_in.shape[1]].set(w_in.astype(BF16))
    lat = _norm_mod_matmul(x, g, shift, scale, w_lat, F32)
    q, k, v = _mla_prep(lat, mla_tables, q_norm_g, kv_norm_g, w_uq, w_ukv, q_nope_g, q_rope_g, k_nope_g,
                        k_rope_g, bsz)
    o = _flash_attention(q, k, v)
    return _matmul_residual(o.reshape(x.shape[0], -1), w_out.astype(BF16), x, gate)


def _ret_layer(x, g, shift, scale, gate, bsz, ret_tables, w_in, norm_g, w_out):
    heads = w_out.shape[0] // RET_V
    proj = _norm_mod_matmul(x, g, shift, scale, w_in.astype(BF16), BF16, tn=1024)
    y = _retention(proj, ret_tables[0], ret_tables[1], norm_g, bsz, heads)
    return _matmul_residual(y, w_out.astype(BF16), x, gate)


def kernel(x, c, positions, ada_w, ada_b, norm1_g, norm2_g, ssd_w_in, ssd_conv_w, ssd_conv_b, ssd_dt_bias, ssd_a_log, ssd_d, ssd_norm_g, ssd_w_out, mla_w_in, mla_q_norm_g, mla_kv_norm_g, mla_w_uq, mla_w_ukv, mla_q_nope_g, mla_q_rope_g, mla_k_nope_g, mla_k_rope_g, mla_w_out, ret_w_in, ret_norm_g, ret_w_out, ffn_w_in, ffn_w_out, router_w, moe_w_in, moe_w_out):
    bsz, s, d = x.shape
    depth = ada_w.shape[0]
    mod = _ada_mod(c, ada_w, ada_b).reshape(depth, bsz, 6, 1, d)
    rc, rs, mc, ma, mb = _rope_tables(positions)
    xt = x.reshape(bsz * s, d)
    moe_wi = moe_w_in.astype(BF16).reshape((-1,) + moe_w_in.shape[2:])
    moe_wo = moe_w_out.astype(BF16).reshape((-1,) + moe_w_out.shape[2:])
    for i in range(depth):
        shift1, scale1, gate1, shift2, scale2, gate2 = (mod[i, :, m] for m in range(6))
        kind, j = i % N_MIXERS, i // N_MIXERS
        if kind == 0:
            xt = _ssd_layer(xt, norm1_g[i], shift1, scale1, gate1, bsz, ssd_w_in[j], ssd_conv_w[j], ssd_conv_b[j],
                            ssd_dt_bias[j], ssd_a_log[j], ssd_d[j], ssd_norm_g[j], ssd_w_out[j])
        elif kind == 1:
            xt = _mla_layer(xt, norm1_g[i], shift1, scale1, gate1, bsz, (mc, ma, mb), mla_w_in[j],
                            mla_q_norm_g[j], mla_kv_norm_g[j], mla_w_uq[j], mla_w_ukv[j], mla_q_nope_g[j],
                            mla_q_rope_g[j], mla_k_nope_g[j], mla_k_rope_g[j], mla_w_out[j])
        else:
            xt = _ret_layer(xt, norm1_g[i], shift1, scale1, gate1, bsz, (rc, rs), ret_w_in[j], ret_norm_g[j],
                            ret_w_out[j])
        if i % 2 == 0:
            xt = _ffn(xt, norm2_g[i], shift2, scale2, gate2, ffn_w_in[i // 2], ffn_w_out[i // 2])
        else:
            xt = _moe(xt, norm2_g[i], shift2, scale2, gate2, router_w[i // 2], moe_wi, moe_wo, i // 2)
    return xt.reshape(bsz, s, d)
```

```python
import functools
import math

import numpy as np
import jax
import jax.numpy as jnp
from jax import lax
from jax.experimental import pallas as pl
from jax.experimental.pallas import tpu as pltpu

F32 = jnp.float32
BF16 = jnp.bfloat16
EPS = 1e-6
LANES = 128
MIB = 1024 * 1024

CHUNK = 64
SSD_HEAD_DIM = 64
SSD_GROUPS = 4
SSD_STATE = 128
SSD_CONV = 4
MLA_HEADS = 8
MLA_NOPE = 128
MLA_ROPE = 64
MLA_V = 128
MLA_Q_RANK = 256
MLA_KV_RANK = 256
ROPE_BASE = 10000.0
RET_QK = 256
RET_V = 512
N_MIXERS = 3
TOP_K = 2

SSD_L = 128
RET_L = 256
ATT_TK = 512
DMA_UNROLL = 16
MOE_ROWS = 512
FFN_TF = 256


def _cparams(sem, vmem_mib):
    return pltpu.CompilerParams(dimension_semantics=sem, vmem_limit_bytes=vmem_mib * MIB)


def _silu(v):
    half = 0.5 * v
    return half + half * jnp.tanh(half)


def _norm_mod(x, g, shift, scale):
    y = x * lax.rsqrt(jnp.mean(x * x, axis=-1, keepdims=True) + EPS)
    return (y * g) * (1.0 + scale) + shift


def _ada_kernel(cb_ref, w_ref, b_ref, o_ref):
    nb = cb_ref.shape[0]
    tn = w_ref.shape[2]
    for jn in range(tn // LANES):
        cols = slice(jn * LANES, (jn + 1) * LANES)
        w = w_ref[0, :, cols]
        for b in range(nb):
            s = jnp.sum(w * cb_ref[b], axis=0, keepdims=True)
            o_ref[0, b:b + 1, cols] = s + b_ref[0, :, cols]


def _ada_mod(c, ada_w, ada_b, tn=1024):
    depth, d, n = ada_w.shape
    bsz = c.shape[0]
    cb = jnp.broadcast_to(c[:, :, None], (bsz, d, LANES))
    return pl.pallas_call(
        _ada_kernel,
        grid=(depth, n // tn),
        in_specs=[pl.BlockSpec((bsz, d, LANES), lambda l, j: (0, 0, 0)),
                  pl.BlockSpec((1, d, tn), lambda l, j: (l, 0, j)),
                  pl.BlockSpec((1, 1, tn), lambda l, j: (l, 0, j))],
        out_specs=pl.BlockSpec((1, bsz, tn), lambda l, j: (l, 0, j)),
        out_shape=jax.ShapeDtypeStruct((depth, bsz, n), F32),
        compiler_params=_cparams(("parallel", "parallel"), 32),
        name="ada_mod",
    )(cb, ada_w, ada_b.reshape(depth, 1, n))


def _nm_matmul_kernel(x_ref, g_ref, sh_ref, sc_ref, w_ref, o_ref, h_ref):
    @pl.when(pl.program_id(1) == 0)
    def _():
        h_ref[...] = _norm_mod(x_ref[...], g_ref[...], sh_ref[0], sc_ref[0]).astype(BF16)

    o_ref[...] = jnp.dot(h_ref[...], w_ref[...], preferred_element_type=F32).astype(o_ref.dtype)


def _norm_mod_matmul(x, g, shift, scale, w, out_dtype, tm=1024, tn=None):
    t, d = x.shape
    n = w.shape[1]
    tn = n if tn is None else tn
    tiles_per_batch = t // shift.shape[0] // tm
    mod_spec = pl.BlockSpec((1, 1, d), lambda i, j: (i // tiles_per_batch, 0, 0))
    return pl.pallas_call(
        _nm_matmul_kernel,
        grid=(t // tm, n // tn),
        in_specs=[pl.BlockSpec((tm, d), lambda i, j: (i, 0)),
                  pl.BlockSpec((1, d), lambda i, j: (0, 0)),
                  mod_spec, mod_spec,
                  pl.BlockSpec((d, tn), lambda i, j: (0, j))],
        out_specs=pl.BlockSpec((tm, tn), lambda i, j: (i, j)),
        out_shape=jax.ShapeDtypeStruct((t, n), out_dtype),
        scratch_shapes=[pltpu.VMEM((tm, d), BF16)],
        compiler_params=_cparams(("parallel", "arbitrary"), 48),
        name="norm_mod_matmul",
    )(x, g.reshape(1, d), shift, scale, w)


def _mm_res_kernel(y_ref, w_ref, x_ref, gate_ref, o_ref):
    o_ref[...] = x_ref[...] + gate_ref[0] * jnp.dot(y_ref[...], w_ref[...], preferred_element_type=F32)


def _matmul_residual(y, w, x, gate, tm=512):
    t, k = y.shape
    d = w.shape[1]
    tiles_per_batch = t // gate.shape[0] // tm
    return pl.pallas_call(
        _mm_res_kernel,
        grid=(t // tm,),
        in_specs=[pl.BlockSpec((tm, k), lambda i: (i, 0)),
                  pl.BlockSpec((k, d), lambda i: (0, 0)),
                  pl.BlockSpec((tm, d), lambda i: (i, 0)),
                  pl.BlockSpec((1, 1, d), lambda i: (i // tiles_per_batch, 0, 0))],
        out_specs=pl.BlockSpec((tm, d), lambda i: (i, 0)),
        out_shape=jax.ShapeDtypeStruct((t, d), F32),
        compiler_params=_cparams(("parallel",), 48),
        name="matmul_residual",
    )(y, w, x, gate)


def _swiglu_accumulate(h_ref, wi_ref, wo_ref, acc_ref, tf, per_chunk=None):
    hidden = wo_ref.shape[0]
    acc_ref[...] = jnp.zeros_like(acc_ref)

    def body(f, carry):
        if per_chunk is not None:
            per_chunk(f)
        lo = pl.multiple_of(f * tf, tf)
        h = h_ref[...]
        gt = jnp.dot(h, wi_ref[:, pl.ds(lo, tf)], preferred_element_type=F32)
        up = jnp.dot(h, wi_ref[:, pl.ds(hidden + lo, tf)], preferred_element_type=F32)
        act = (_silu(gt) * up).astype(BF16)
        acc_ref[...] += jnp.dot(act, wo_ref[pl.ds(lo, tf), :], preferred_element_type=F32)
        return carry

    lax.fori_loop(0, hidden // tf, body, 0, unroll=True)


def _ffn_kernel(x_ref, g_ref, sh_ref, sc_ref, gate_ref, wi_ref, wo_ref, o_ref, h_ref, acc_ref, *, tf):
    x = x_ref[...]
    h_ref[...] = _norm_mod(x, g_ref[...], sh_ref[0], sc_ref[0]).astype(BF16)
    _swiglu_accumulate(h_ref, wi_ref, wo_ref, acc_ref, tf)
    o_ref[...] = x + gate_ref[0] * acc_ref[...]


def _ffn(x, g, shift, scale, gate, w_in, w_out, tm=512, tf=FFN_TF):
    t, d = x.shape
    wi, wo = w_in.astype(BF16), w_out.astype(BF16)
    hidden = wo.shape[0]
    tiles_per_batch = t // gate.shape[0] // tm
    mod_spec = pl.BlockSpec((1, 1, d), lambda i: (i // tiles_per_batch, 0, 0))
    resident = pl.Buffered(1)
    return pl.pallas_call(
        functools.partial(_ffn_kernel, tf=tf),
        grid=(t // tm,),
        in_specs=[pl.BlockSpec((tm, d), lambda i: (i, 0)),
                  pl.BlockSpec((1, d), lambda i: (0, 0)),
                  mod_spec, mod_spec, mod_spec,
                  pl.BlockSpec((d, 2 * hidden), lambda i: (0, 0), pipeline_mode=resident),
                  pl.BlockSpec((hidden, d), lambda i: (0, 0), pipeline_mode=resident)],
        out_specs=pl.BlockSpec((tm, d), lambda i: (i, 0)),
        out_shape=jax.ShapeDtypeStruct((t, d), F32),
        scratch_shapes=[pltpu.VMEM((tm, d), BF16), pltpu.VMEM((tm, d), F32)],
        compiler_params=_cparams(("parallel",), 48),
        name="ffn",
    )(x, g.reshape(1, d), shift, scale, gate, wi, wo)


def _rope_kernel(pos_ref, inv_ret_ref, inv_mla_ref, rc_ref, rs_ref, mc_ref, ma_ref, mb_ref):
    pos = pos_ref[...]
    ang = pos * inv_ret_ref[...]
    rc_ref[...] = jnp.cos(ang)
    rs_ref[...] = jnp.sin(ang)
    half = MLA_ROPE // 2
    ang = pos * inv_mla_ref[...]
    lane = lax.broadcasted_iota(jnp.int32, ang.shape, 1)
    c = jnp.cos(ang)
    s = jnp.sin(ang)
    mc_ref[...] = jnp.where(lane < MLA_ROPE, c, 0.0)
    ma_ref[...] = jnp.where(lane < half, -s, 0.0)
    mb_ref[...] = jnp.where((lane >= half) & (lane < MLA_ROPE), s, 0.0)


def _rope_tables(positions, tm=1024):
    t = positions.size
    pos = positions.astype(F32).reshape(t, 1)
    inv_ret = ROPE_BASE ** (-jnp.arange(0, RET_QK, 2, dtype=F32) / RET_QK)
    inv_half = ROPE_BASE ** (-jnp.arange(0, MLA_ROPE, 2, dtype=F32) / MLA_ROPE)
    inv_mla = jnp.concatenate([inv_half, inv_half, jnp.zeros((LANES - MLA_ROPE,), F32)])
    row = pl.BlockSpec((tm, LANES), lambda i: (i, 0))
    const = pl.BlockSpec((1, LANES), lambda i: (0, 0))
    out = jax.ShapeDtypeStruct((t, LANES), F32)
    return pl.pallas_call(
        _rope_kernel,
        grid=(t // tm,),
        in_specs=[pl.BlockSpec((tm, 1), lambda i: (i, 0)), const, const],
        out_specs=[row] * 5,
        out_shape=[out] * 5,
        compiler_params=_cparams(("parallel",), 32),
        name="rope_tables",
    )(pos, inv_ret.reshape(1, LANES), inv_mla.reshape(1, LANES))


def _softplus(v):
    return jnp.maximum(v, 0.0) + jnp.log(1.0 + jnp.exp(-jnp.abs(v)))


def _ssd_kernel(zx_ref, dt_ref, cw_ref, cb_ref, dtb_ref, ah_ref, dsk_ref, ng_ref, o_ref,
                ext_ref, act_ref, st_ref, acum_ref, acumT_ref, dtT_ref, wcol_ref, y_ref, xw_ref, dec_ref,
                *, d_inner, groups, n_state, head_dim):
    L = zx_ref.shape[0]
    conv_dim = act_ref.shape[1]
    heads_per_group = d_inner // groups // head_dim
    gw = d_inner // groups
    pairs = gw // LANES

    @pl.when(pl.program_id(1) == 0)
    def _():
        ext_ref[0:8, :] = jnp.zeros((8, conv_dim), F32)
        st_ref[...] = jnp.zeros_like(st_ref)

    ext_ref[8:8 + L, :] = zx_ref[:, d_inner:d_inner + conv_dim].astype(F32)
    conv = cb_ref[...]
    for k in range(SSD_CONV):
        conv = conv + cw_ref[k:k + 1, :] * ext_ref[5 + k:5 + k + L, :]
    act_ref[...] = _silu(conv)
    ext_ref[0:8, :] = ext_ref[L:L + 8, :]

    dt = _softplus(dt_ref[...] + dtb_ref[...])
    a = dt * ah_ref[...]
    row = lax.broadcasted_iota(jnp.int32, (L, L), 0)
    col = lax.broadcasted_iota(jnp.int32, (L, L), 1)
    tril = row >= col
    a_cum = jnp.dot(tril.astype(F32), a, precision=lax.Precision.HIGHEST, preferred_element_type=F32)
    a_last = a_cum[L - 1:L, :]
    acum_ref[...] = a_cum
    acumT_ref[...] = a_cum.T
    dtT_ref[...] = dt.T
    wcol_ref[...] = dt * jnp.exp(a_last - a_cum)

    lane = lax.broadcasted_iota(jnp.int32, (L, LANES), 1)
    lane_lo = lane < head_dim
    lane_lo_row = lane_lo[0:1, :]

    for g in range(groups):
        b_g = act_ref[:, d_inner + g * n_state:d_inner + (g + 1) * n_state].astype(BF16)
        c_g = act_ref[:, d_inner + (groups + g) * n_state:d_inner + (groups + g + 1) * n_state].astype(BF16)
        cb = lax.dot_general(c_g, b_g, (((1,), (1,)), ((), ())), preferred_element_type=F32)
        y_off = jnp.dot(c_g, st_ref[g].astype(BF16), preferred_element_type=F32)
        for p in range(pairs):
            cols = slice(g * gw + p * LANES, g * gw + (p + 1) * LANES)
            xs_pair = act_ref[:, cols]
            xs_bf = xs_pair.astype(BF16)
            res, e_l, w_l = [], [], []
            for hh in range(LANES // head_dim):
                h = g * heads_per_group + p * (LANES // head_dim) + hh
                a_l = jnp.broadcast_to(acum_ref[:, h:h + 1], (L, L))
                m = jnp.where(tril, jnp.exp(a_l - acumT_ref[h:h + 1, :]), 0.0) * cb * dtT_ref[h:h + 1, :]
                res.append(jnp.dot(m.astype(BF16), xs_bf, preferred_element_type=F32))
                e_l.append(jnp.exp(jnp.broadcast_to(acum_ref[:, h:h + 1], (L, LANES))))
                w_l.append(jnp.broadcast_to(wcol_ref[:, h:h + 1], (L, LANES)))
            yo = y_off[:, p * LANES:(p + 1) * LANES]
            y_ref[:, cols] = (jnp.where(lane_lo, res[0] + e_l[0] * yo, res[1] + e_l[1] * yo)
                              + xs_pair * dsk_ref[:, cols])
            xw_ref[:, p * LANES:(p + 1) * LANES] = (xs_pair * jnp.where(lane_lo, w_l[0], w_l[1])).astype(BF16)
            dec_ref[:, p * LANES:(p + 1) * LANES] = jnp.where(lane_lo_row, e_l[0][L - 1:L, :], e_l[1][L - 1:L, :])
        upd = lax.dot_general(b_g, xw_ref[...], (((0,), (0,)), ((), ())), preferred_element_type=F32)
        st_ref[g] = st_ref[g] * dec_ref[...] + upd

    for g in range(groups):
        cols = slice(g * gw, (g + 1) * gw)
        yg = y_ref[:, cols] * _silu(zx_ref[:, cols].astype(F32))
        yg = yg * lax.rsqrt(jnp.mean(yg * yg, axis=-1, keepdims=True) + EPS)
        o_ref[:, cols] = (yg * ng_ref[:, cols]).astype(o_ref.dtype)


def _ssd_scan(zx, dt_raw, conv_w, conv_b, dt_bias, a_log, d_skip, norm_g, bsz, L=SSD_L):
    t = zx.shape[0]
    s = t // bsz
    heads = dt_bias.shape[0]
    d_inner = heads * SSD_HEAD_DIM
    conv_dim = conv_w.shape[1]
    gw = d_inner // SSD_GROUPS
    steps = s // L

    def pad_heads(v):
        return jnp.zeros((1, LANES), F32).at[0, :heads].set(v.astype(F32))

    a_head = pad_heads(-jnp.exp(a_log.astype(F32)))
    d_exp = jnp.repeat(d_skip.astype(F32), SSD_HEAD_DIM).reshape(1, d_inner)
    row = lambda w: pl.BlockSpec((L, w), lambda b, j: (b * steps + j, 0))
    const = lambda r, w: pl.BlockSpec((r, w), lambda b, j: (0, 0))
    kern = functools.partial(_ssd_kernel, d_inner=d_inner, groups=SSD_GROUPS, n_state=SSD_STATE,
                             head_dim=SSD_HEAD_DIM)
    return pl.pallas_call(
        kern,
        grid=(bsz, steps),
        in_specs=[row(zx.shape[1]), row(LANES), const(SSD_CONV, conv_dim), const(1, conv_dim),
                  const(1, LANES), const(1, LANES), const(1, d_inner), const(1, d_inner)],
        out_specs=row(d_inner),
        out_shape=jax.ShapeDtypeStruct((t, d_inner), BF16),
        scratch_shapes=[pltpu.VMEM((L + 8, conv_dim), F32),
                        pltpu.VMEM((L, conv_dim), F32),
                        pltpu.VMEM((SSD_GROUPS, SSD_STATE, gw), F32),
                        pltpu.VMEM((L, LANES), F32),
                        pltpu.VMEM((LANES, L), F32),
                        pltpu.VMEM((LANES, L), F32),
                        pltpu.VMEM((L, LANES), F32),
                        pltpu.VMEM((L, d_inner), F32),
                        pltpu.VMEM((L, gw), BF16),
                        pltpu.VMEM((1, gw), F32)],
        compiler_params=_cparams(("parallel", "arbitrary"), 48),
        name="ssd_scan",
    )(zx, dt_raw, conv_w.astype(F32), conv_b.astype(F32).reshape(1, conv_dim), pad_heads(dt_bias), a_head,
      d_exp, norm_g.astype(F32).reshape(1, d_inner))


def _rms_rows(v, n):
    return v * lax.rsqrt(jnp.sum(v * v, axis=-1, keepdims=True) / n + EPS)


def _mla_prep_kernel(lat_ref, qg_ref, kvg_ref, wqn_ref, wqp_ref, wkv_ref, gqn_ref, gqp_ref, gkn_ref, gkp_ref,
                     mc_ref, ma_ref, mb_ref, q_ref, k_ref, v_ref, *, scale):
    lat = lat_ref[...]
    cq = lat[:, :MLA_Q_RANK]
    ckv = lat[:, MLA_Q_RANK:MLA_Q_RANK + MLA_KV_RANK]
    kpe = lat[:, MLA_Q_RANK + MLA_KV_RANK:]
    qn = (_rms_rows(cq, MLA_Q_RANK) * qg_ref[...]).astype(BF16)
    kvn = (_rms_rows(ckv, MLA_KV_RANK) * kvg_ref[...]).astype(BF16)
    q_nope = jnp.dot(qn, wqn_ref[...], preferred_element_type=F32)
    q_pe = jnp.dot(qn, wqp_ref[...], preferred_element_type=F32)
    kv = jnp.dot(kvn, wkv_ref[...], preferred_element_type=F32)
    mc, ma, mb = mc_ref[...], ma_ref[...], mb_ref[...]

    def rotate(v):
        return v * mc + pltpu.roll(v, LANES - MLA_ROPE // 2, 1) * ma + pltpu.roll(v, MLA_ROPE // 2, 1) * mb

    k_pe = rotate(_rms_rows(kpe, MLA_ROPE) * gkp_ref[...]).astype(BF16)
    for h in range(MLA_HEADS):
        qn_h = _rms_rows(q_nope[:, h * LANES:(h + 1) * LANES], MLA_NOPE) * gqn_ref[...]
        qp_h = rotate(_rms_rows(q_pe[:, h * LANES:(h + 1) * LANES], MLA_ROPE) * gqp_ref[...])
        q_ref[0, h, :, 0:LANES] = (qn_h * scale).astype(BF16)
        q_ref[0, h, :, LANES:2 * LANES] = (qp_h * scale).astype(BF16)
        kn_h = _rms_rows(kv[:, 2 * h * LANES:(2 * h + 1) * LANES], MLA_NOPE) * gkn_ref[...]
        k_ref[0, h, :, 0:LANES] = kn_h.astype(BF16)
        k_ref[0, h, :, LANES:2 * LANES] = k_pe
        v_ref[0, h] = kv[:, (2 * h + 1) * LANES:(2 * h + 2) * LANES].astype(BF16)


def _pad_lanes(v, n=LANES):
    return jnp.zeros((1, n), F32).at[0, :v.shape[0]].set(v.astype(F32))


def _mla_prep(lat, tables, q_norm_g, kv_norm_g, w_uq, w_ukv, q_nope_g, q_rope_g, k_nope_g, k_rope_g, bsz, tm=256):
    t = lat.shape[0]
    s = t // bsz
    steps = s // tm
    hh = MLA_HEADS
    w_uq = w_uq.reshape(MLA_Q_RANK, hh, MLA_NOPE + MLA_ROPE)
    wqn = w_uq[:, :, :MLA_NOPE].reshape(MLA_Q_RANK, hh * MLA_NOPE).astype(BF16)
    wqp = jnp.pad(w_uq[:, :, MLA_NOPE:], ((0, 0), (0, 0), (0, LANES - MLA_ROPE)))
    wqp = wqp.reshape(MLA_Q_RANK, hh * LANES).astype(BF16)
    wkv = w_ukv.astype(BF16)
    mc, ma, mb = tables
    row = pl.BlockSpec((tm, LANES), lambda b, i: (b * steps + i, 0))
    const = lambda r, w: pl.BlockSpec((r, w), lambda b, i: (0, 0))
    head_out = lambda w: pl.BlockSpec((1, hh, tm, w), lambda b, i: (b, 0, i, 0))
    return pl.pallas_call(
        functools.partial(_mla_prep_kernel, scale=(MLA_NOPE + MLA_ROPE) ** -0.5),
        grid=(bsz, steps),
        in_specs=[pl.BlockSpec((tm, lat.shape[1]), lambda b, i: (b * steps + i, 0)),
                  const(1, MLA_Q_RANK), const(1, MLA_KV_RANK),
                  const(MLA_Q_RANK, hh * LANES), const(MLA_Q_RANK, hh * LANES), const(MLA_KV_RANK, hh * 2 * LANES),
                  const(1, LANES), const(1, LANES), const(1, LANES), const(1, LANES),
                  row, row, row],
        out_specs=[head_out(2 * LANES), head_out(2 * LANES), head_out(LANES)],
        out_shape=[jax.ShapeDtypeStruct((bsz, hh, s, 2 * LANES), BF16),
                   jax.ShapeDtypeStruct((bsz, hh, s, 2 * LANES), BF16),
                   jax.ShapeDtypeStruct((bsz, hh, s, LANES), BF16)],
        compiler_params=_cparams(("parallel", "parallel"), 48),
        name="mla_prep",
    )(lat, _pad_lanes(q_norm_g, MLA_Q_RANK), _pad_lanes(kv_norm_g, MLA_KV_RANK), wqn, wqp, wkv,
      _pad_lanes(q_nope_g), _pad_lanes(q_rope_g), _pad_lanes(k_nope_g), _pad_lanes(k_rope_g), mc, ma, mb)


def _flash_kernel(q_ref, k_ref, v_ref, o_ref, s_ref, m_ref, l_ref, acc_ref, *, tq, tk):
    i = pl.program_id(2)
    n_strips = tq // CHUNK
    chunks_per_tile = tk // CHUNK

    def scores(j, slot):
        kt = k_ref[0, 0, pl.ds(pl.multiple_of(j * tk, tk), tk), :]
        s_ref[slot] = lax.dot_general(q_ref[0, 0], kt, (((1,), (1,)), ((), ())), preferred_element_type=F32)

    def update(j, slot, diag=None):
        lane = lax.broadcasted_iota(jnp.int32, (CHUNK, LANES), 1)
        m_all, l_all = m_ref[...], l_ref[...]
        m_out, l_out, a_out, p_out = [], [], [], []
        for r in range(n_strips):
            rows = slice(r * CHUNK, (r + 1) * CHUNK)
            m_prev, l_prev = m_all[rows], l_all[rows]
            visible = tk if diag is None else min(max((r - diag * chunks_per_tile + 1) * CHUNK, 0), tk)
            n_blocks = -(-visible // LANES)
            hidden_cols = [jnp.zeros((CHUNK, tk - n_blocks * LANES), BF16)] if n_blocks < tk // LANES else []
            if visible == 0:
                m_out.append(m_prev)
                l_out.append(l_prev)
                a_out.append(jnp.ones((CHUNK, LANES), F32))
                p_out.append(hidden_cols[0])
                continue
            blocks = [s_ref[slot, rows, c * LANES:(c + 1) * LANES] for c in range(n_blocks)]
            if visible % LANES:
                blocks[-1] = jnp.where(lane < visible % LANES, blocks[-1], -jnp.inf)
            row_max = jnp.max(functools.reduce(jnp.maximum, blocks), axis=-1, keepdims=True)
            m_new = jnp.maximum(m_prev, jnp.broadcast_to(row_max, (CHUNK, LANES)))
            alpha = jnp.exp(m_prev - m_new)
            ps = [jnp.exp(b - m_new) for b in blocks]
            p_out.append(jnp.concatenate([p.astype(BF16) for p in ps] + hidden_cols, axis=1))
            m_out.append(m_new)
            l_out.append(alpha * l_prev + functools.reduce(jnp.add, ps))
            a_out.append(alpha)
        m_ref[...] = jnp.concatenate(m_out, axis=0)
        l_ref[...] = jnp.concatenate(l_out, axis=0)
        vt = v_ref[0, 0, pl.ds(pl.multiple_of(j * tk, tk), tk), :]
        acc_ref[...] = (jnp.concatenate(a_out, axis=0) * acc_ref[...]
                        + jnp.dot(jnp.concatenate(p_out, axis=0), vt, preferred_element_type=F32))

    m_ref[...] = jnp.full_like(m_ref, -jnp.inf)
    l_ref[...] = jnp.zeros_like(l_ref)
    acc_ref[...] = jnp.zeros_like(acc_ref)
    scores(0, 0)

    def body(t, carry):
        scores(2 * t + 1, 1)
        update(2 * t, 0)
        scores(2 * t + 2, 0)
        update(2 * t + 1, 1)
        return carry

    lax.fori_loop(0, i, body, 0)
    scores(2 * i + 1, 1)
    update(2 * i, 0, diag=0)
    update(2 * i + 1, 1, diag=1)
    o_ref[0] = (acc_ref[...] / jnp.sum(l_ref[...], axis=-1, keepdims=True)).astype(o_ref.dtype)


def _flash_attention(q, k, v, tk=ATT_TK):
    bsz, hh, s, dq = q.shape
    dv = v.shape[-1]
    tq = 2 * tk
    return pl.pallas_call(
        functools.partial(_flash_kernel, tq=tq, tk=tk),
        grid=(bsz, hh, s // tq),
        in_specs=[pl.BlockSpec((1, 1, tq, dq), lambda b, h, i: (b, h, i, 0)),
                  pl.BlockSpec((1, 1, s, dq), lambda b, h, i: (b, h, 0, 0)),
                  pl.BlockSpec((1, 1, s, dv), lambda b, h, i: (b, h, 0, 0))],
        out_specs=pl.BlockSpec((1, tq, dv), lambda b, h, i: (b, i, h)),
        out_shape=jax.ShapeDtypeStruct((bsz, s, hh * dv), BF16),
        scratch_shapes=[pltpu.VMEM((2, tq, tk), F32),
                        pltpu.VMEM((tq, LANES), F32),
                        pltpu.VMEM((tq, LANES), F32),
                        pltpu.VMEM((tq, dv), F32)],
        compiler_params=_cparams(("parallel", "parallel", "arbitrary"), 48),
        name="flash_attention",
    )(q, k, v)


def _ret_kernel(p_ref, cos_ref, sin_ref, ng_ref, o_ref, st_ref, dm_ref, qd_ref, kd_ref, *, heads, log_gamma):
    L = p_ref.shape[0]
    dk, dv = RET_QK, RET_V
    half = dk // 2

    @pl.when(pl.program_id(1) == 0)
    def _():
        st_ref[...] = jnp.zeros_like(st_ref)
        row = lax.broadcasted_iota(jnp.int32, (L, L), 0)
        col = lax.broadcasted_iota(jnp.int32, (L, L), 1)
        visible = (col // CHUNK) <= (row // CHUNK)
        dist = jnp.abs(row - col).astype(F32)
        pos = lax.broadcasted_iota(jnp.int32, (L, LANES), 0).astype(F32)
        for h in range(heads):
            dm_ref[h] = jnp.where(visible, jnp.exp(log_gamma[h] * dist), 0.0)
            qd_ref[h] = jnp.exp(log_gamma[h] * (pos + 1.0))
            kd_ref[h] = jnp.exp(log_gamma[h] * (L - 1.0 - pos))

    c = cos_ref[...]
    s = sin_ref[...]
    k_scale = dk ** -0.5
    for h in range(heads):
        q1 = p_ref[:, h * dk:h * dk + half].astype(F32)
        q2 = p_ref[:, h * dk + half:(h + 1) * dk].astype(F32)
        k1 = p_ref[:, (heads + h) * dk:(heads + h) * dk + half].astype(F32)
        k2 = p_ref[:, (heads + h) * dk + half:(heads + h + 1) * dk].astype(F32)
        v = p_ref[:, 2 * heads * dk + h * dv:2 * heads * dk + (h + 1) * dv]
        gate = p_ref[:, 2 * heads * dk + heads * dv + h * dv:2 * heads * dk + heads * dv + (h + 1) * dv]
        qr1, qr2 = q1 * c - q2 * s, q1 * s + q2 * c
        kr1, kr2 = (k1 * c - k2 * s) * k_scale, (k1 * s + k2 * c) * k_scale
        qd, kd = qd_ref[h], kd_ref[h]
        q_bf = jnp.concatenate([qr1, qr2], axis=-1).astype(BF16)
        k_bf = jnp.concatenate([kr1, kr2], axis=-1).astype(BF16)
        q_in = jnp.concatenate([qr1 * qd, qr2 * qd], axis=-1).astype(BF16)
        k_out = jnp.concatenate([kr1 * kd, kr2 * kd], axis=-1).astype(BF16)
        sc = lax.dot_general(q_bf, k_bf, (((1,), (1,)), ((), ())), preferred_element_type=F32) * dm_ref[h]
        y = (jnp.dot(sc.astype(BF16), v, preferred_element_type=F32)
             + jnp.dot(q_in, st_ref[h].astype(BF16), preferred_element_type=F32))
        upd = lax.dot_general(k_out, v, (((0,), (0,)), ((), ())), preferred_element_type=F32)
        st_ref[h] = st_ref[h] * math.exp(log_gamma[h] * L) + upd
        y = y * lax.rsqrt(jnp.mean(y * y, axis=-1, keepdims=True) + EPS) * ng_ref[:, h * dv:(h + 1) * dv]
        o_ref[:, h * dv:(h + 1) * dv] = (_silu(gate.astype(F32)) * y).astype(o_ref.dtype)


def _retention(proj, cos, sin, norm_g, bsz, heads, L=RET_L):
    t = proj.shape[0]
    steps = t // bsz // L
    log_gamma = tuple(math.log1p(-2.0 ** (-5.0 - h)) for h in range(heads))
    row = lambda w: pl.BlockSpec((L, w), lambda b, j: (b * steps + j, 0))
    return pl.pallas_call(
        functools.partial(_ret_kernel, heads=heads, log_gamma=log_gamma),
        grid=(bsz, steps),
        in_specs=[row(proj.shape[1]), row(LANES), row(LANES),
                  pl.BlockSpec((1, heads * RET_V), lambda b, j: (0, 0))],
        out_specs=row(heads * RET_V),
        out_shape=jax.ShapeDtypeStruct((t, heads * RET_V), BF16),
        scratch_shapes=[pltpu.VMEM((heads, RET_QK, RET_V), F32),
                        pltpu.VMEM((heads, L, L), F32),
                        pltpu.VMEM((heads, L, LANES), F32),
                        pltpu.VMEM((heads, L, LANES), F32)],
        compiler_params=_cparams(("parallel", "arbitrary"), 48),
        name="retention",
    )(proj, cos, sin, norm_g.astype(F32).reshape(1, heads * RET_V))


def _router_kernel(x_ref, g_ref, sh_ref, sc_ref, rw_ref, h_ref, idx_ref, gates_ref, *, n_experts):
    h = _norm_mod(x_ref[...], g_ref[...], sh_ref[0], sc_ref[0])
    h_ref[...] = h
    logits = jnp.dot(h, rw_ref[...], precision=lax.Precision.HIGHEST, preferred_element_type=F32)
    lane = lax.broadcasted_iota(jnp.int32, logits.shape, 1).astype(F32)
    lg = jnp.where(lane < n_experts, logits, -jnp.inf)
    m1 = jnp.max(lg, axis=-1, keepdims=True)
    i1 = jnp.min(jnp.where(lg == m1, lane, float(LANES)), axis=-1, keepdims=True)
    lg2 = jnp.where(lane == i1, -jnp.inf, lg)
    m2 = jnp.max(lg2, axis=-1, keepdims=True)
    i2 = jnp.min(jnp.where(lg2 == m2, lane, float(LANES)), axis=-1, keepdims=True)
    e2 = jnp.exp(m2 - m1)
    idx_ref[:, 0:1] = i1.astype(jnp.int32)
    idx_ref[:, 1:2] = i2.astype(jnp.int32)
    gates_ref[:, 0:1] = 1.0 / (1.0 + e2)
    gates_ref[:, 1:2] = e2 / (1.0 + e2)


def _router(x, g, shift, scale, router_w, tm=512):
    t, d = x.shape
    n_experts = router_w.shape[1]
    rw = jnp.zeros((d, LANES), F32).at[:, :n_experts].set(router_w.astype(F32))
    tiles_per_batch = t // shift.shape[0] // tm
    mod_spec = pl.BlockSpec((1, 1, d), lambda i: (i // tiles_per_batch, 0, 0))
    return pl.pallas_call(
        functools.partial(_router_kernel, n_experts=n_experts),
        grid=(t // tm,),
        in_specs=[pl.BlockSpec((tm, d), lambda i: (i, 0)),
                  pl.BlockSpec((1, d), lambda i: (0, 0)),
                  mod_spec, mod_spec,
                  pl.BlockSpec((d, LANES), lambda i: (0, 0))],
        out_specs=[pl.BlockSpec((tm, d), lambda i: (i, 0)),
                   pl.BlockSpec((tm, TOP_K), lambda i: (i, 0)),
                   pl.BlockSpec((tm, TOP_K), lambda i: (i, 0))],
        out_shape=[jax.ShapeDtypeStruct((t, d), F32),
                   jax.ShapeDtypeStruct((t, TOP_K), jnp.int32),
                   jax.ShapeDtypeStruct((t, TOP_K), F32)],
        compiler_params=_cparams(("parallel",), 48),
        name="router",
    )(x, g.reshape(1, d), shift, scale, rw)


def _row_copy(src_ref, dst_ref, src_row, dst_row, sem):
    return pltpu.make_async_copy(src_ref.at[pl.ds(src_row, 1)], dst_ref.at[pl.ds(dst_row, 1)], sem)


def _start_row_copies(tok_ref, src_ref, dst_ref, sem, first, count):
    for u in range(count):
        r = first + u
        _row_copy(src_ref, dst_ref, tok_ref[0, 0, r], r, sem).start(priority=u % 2)


def _wait_row_copies(src_ref, dst_ref, sem):
    def wait(r, carry):
        _row_copy(src_ref, dst_ref, 0, r, sem).wait()
        return carry

    lax.fori_loop(0, dst_ref.shape[0], wait, 0, unroll=DMA_UNROLL)


def _expert_kernel(bexp_ref, bvalid_ref, tok_first_ref, tok_next_ref, h_hbm, wi_ref, wo_ref, o_ref,
                   xbuf, h_ref, acc_ref, sem, *, tf):
    b = pl.program_id(0)
    last = pl.num_programs(0) - 1
    slot = b % 2
    valid = bvalid_ref[b] != 0
    rows = h_ref.shape[0]
    n_chunks = wo_ref.shape[1] // tf
    per = rows // n_chunks
    next_buf, next_sem = xbuf.at[1 - slot], sem.at[1 - slot]

    def start_all(tok_ref, buf, buf_sem):
        def start(r, carry):
            _start_row_copies(tok_ref, h_hbm, buf, buf_sem, r * DMA_UNROLL, DMA_UNROLL)
            return carry

        lax.fori_loop(0, rows // DMA_UNROLL, start, 0)

    @pl.when(b == 0)
    def _():
        start_all(tok_first_ref, xbuf.at[0], sem.at[0])

    _wait_row_copies(h_hbm, xbuf.at[slot], sem.at[slot])

    @pl.when(valid)
    def _():
        h_ref[...] = xbuf[slot].astype(BF16)
        _start_row_copies(tok_next_ref, h_hbm, next_buf, next_sem, per * n_chunks, rows - per * n_chunks)
        _swiglu_accumulate(h_ref, wi_ref.at[0], wo_ref.at[0], acc_ref, tf,
                           per_chunk=lambda f: _start_row_copies(tok_next_ref, h_hbm, next_buf, next_sem,
                                                                 per * f, per))
        o_ref[...] = acc_ref[...]

    @pl.when(jnp.logical_not(valid))
    def _():
        start_all(tok_next_ref, next_buf, next_sem)
        o_ref[...] = jnp.zeros_like(o_ref)

    @pl.when(b == last)
    def _():
        _wait_row_copies(h_hbm, next_buf, next_sem)


def _expert_ffn(h, slot_tok, block_exp, block_valid, wi, wo, rows=MOE_ROWS, tf=FFN_TF):
    d = h.shape[1]
    hidden = wo.shape[1]
    n_blocks = slot_tok.shape[0] // rows
    tok_spec = lambda index: pl.BlockSpec((1, 1, rows), index, memory_space=pltpu.SMEM)
    grid_spec = pltpu.PrefetchScalarGridSpec(
        num_scalar_prefetch=2,
        grid=(n_blocks,),
        in_specs=[tok_spec(lambda b, be, bv: (0, 0, 0)),
                  tok_spec(lambda b, be, bv: (jnp.minimum(b + 1, n_blocks - 1), 0, 0)),
                  pl.BlockSpec(memory_space=pl.ANY),
                  pl.BlockSpec((1, d, 2 * hidden), lambda b, be, bv: (be[b], 0, 0)),
                  pl.BlockSpec((1, hidden, d), lambda b, be, bv: (be[b], 0, 0),
                               pipeline_mode=pl.Buffered(1))],
        out_specs=pl.BlockSpec((rows, d), lambda b, be, bv: (b, 0)),
        scratch_shapes=[pltpu.VMEM((2, rows, d), F32),
                        pltpu.VMEM((rows, d), BF16), pltpu.VMEM((rows, d), F32),
                        pltpu.SemaphoreType.DMA((2,))],
    )
    tok = slot_tok.reshape(n_blocks, 1, rows)
    return pl.pallas_call(
        functools.partial(_expert_kernel, tf=tf),
        grid_spec=grid_spec,
        out_shape=jax.ShapeDtypeStruct((n_blocks * rows, d), F32),
        compiler_params=_cparams(("arbitrary",), 56),
        name="expert_ffn",
    )(block_exp, block_valid, tok, tok, h, wi, wo)


def _combine_kernel(d0_ref, d1_ref, gates_ref, x_ref, gate_ref, yb_ref, o_ref, buf0, buf1, sem):
    rows = o_ref.shape[0]

    def start(r, carry):
        _row_copy(yb_ref, buf0, d0_ref[0, 0, r], r, sem.at[0]).start(priority=0)
        _row_copy(yb_ref, buf1, d1_ref[0, 0, r], r, sem.at[1]).start(priority=1)
        return carry

    def wait(r, carry):
        _row_copy(yb_ref, buf0, 0, r, sem.at[0]).wait()
        _row_copy(yb_ref, buf1, 0, r, sem.at[1]).wait()
        return carry

    lax.fori_loop(0, rows, start, 0, unroll=DMA_UNROLL)
    lax.fori_loop(0, rows, wait, 0, unroll=DMA_UNROLL)
    gts = gates_ref[...]
    y = gts[:, 0:1] * buf0[...] + gts[:, 1:2] * buf1[...]
    o_ref[...] = x_ref[...] + gate_ref[0] * y


def _moe_combine(x, gate, yb, dest, gates, tm=256):
    t, d = x.shape
    tiles_per_batch = t // gate.shape[0] // tm
    slot = pl.BlockSpec((1, 1, tm), lambda i: (i, 0, 0), memory_space=pltpu.SMEM)
    return pl.pallas_call(
        _combine_kernel,
        grid=(t // tm,),
        in_specs=[slot, slot,
                  pl.BlockSpec((tm, TOP_K), lambda i: (i, 0)),
                  pl.BlockSpec((tm, d), lambda i: (i, 0)),
                  pl.BlockSpec((1, 1, d), lambda i: (i // tiles_per_batch, 0, 0)),
                  pl.BlockSpec(memory_space=pl.ANY)],
        out_specs=pl.BlockSpec((tm, d), lambda i: (i, 0)),
        out_shape=jax.ShapeDtypeStruct((t, d), F32),
        scratch_shapes=[pltpu.VMEM((tm, d), F32), pltpu.VMEM((tm, d), F32), pltpu.SemaphoreType.DMA((2,))],
        compiler_params=_cparams(("arbitrary",), 32),
        name="moe_combine",
    )(dest[:, 0].reshape(t // tm, 1, tm), dest[:, 1].reshape(t // tm, 1, tm), gates, x, gate, yb)


def _route(top_idx, n_experts, rows):
    t = top_idx.shape[0]
    n_assign = t * TOP_K
    e_flat = top_idx.reshape(-1)
    onehot = (e_flat[:, None] == jnp.arange(n_experts, dtype=jnp.int32)[None, :]).astype(jnp.int32)
    csum = jnp.cumsum(onehot, axis=0)
    rank = jnp.sum((csum - onehot) * onehot, axis=-1)
    counts = csum[-1]
    padded = (counts + rows - 1) // rows * rows
    pend = jnp.cumsum(padded)
    pstart = pend - padded
    dest = (pstart[e_flat] + rank).astype(jnp.int32)
    n_blocks = n_assign // rows + n_experts
    tok_flat = jnp.repeat(jnp.arange(t, dtype=jnp.int32), TOP_K)
    slot_tok = jnp.zeros((n_blocks * rows,), jnp.int32).at[dest].set(tok_flat)
    block_start = jnp.arange(n_blocks, dtype=jnp.int32) * rows
    block_exp = jnp.clip(jnp.searchsorted(pend, block_start, side="right"), 0, n_experts - 1).astype(jnp.int32)
    block_valid = (block_start < pend[-1]).astype(jnp.int32)
    return slot_tok, dest.reshape(t, TOP_K), block_exp, block_valid


def _moe(x, g, shift, scale, gate, router_w, wi_all, wo_all, layer):
    n_experts = router_w.shape[1]
    h, top_idx, gates = _router(x, g, shift, scale, router_w)
    slot_tok, dest, block_exp, block_valid = _route(top_idx, n_experts, MOE_ROWS)
    yb = _expert_ffn(h, slot_tok, block_exp + layer * n_experts, block_valid, wi_all, wo_all)
    return _moe_combine(x, gate, yb, dest, gates)


def _ssd_layer(x, g, shift, scale, gate, bsz, w_in, conv_w, conv_b, dt_bias, a_log, d_skip, norm_g, w_out):
    heads = dt_bias.shape[0]
    d_inner = heads * SSD_HEAD_DIM
    conv_dim = conv_w.shape[1]
    n_main = d_inner + conv_dim
    w_main = w_in[:, :n_main].astype(BF16)
    w_dt = jnp.zeros((w_in.shape[0], LANES), BF16).at[:, :heads].set(w_in[:, n_main:].astype(BF16))
    zx = _norm_mod_matmul(x, g, shift, scale, w_main, BF16, tn=1024)
    dt_raw = _norm_mod_matmul(x, g, shift, scale, w_dt, F32)
    y = _ssd_scan(zx, dt_raw, conv_w, conv_b, dt_bias, a_log, d_skip, norm_g, bsz)
    return _matmul_residual(y, w_out.astype(BF16), x, gate)


def _mla_layer(x, g, shift, scale, gate, bsz, mla_tables, w_in, q_norm_g, kv_norm_g, w_uq, w_ukv,
               q_nope_g, q_rope_g, k_nope_g, k_rope_g, w_out):
    lat_dim = MLA_Q_RANK + MLA_KV_RANK + LANES
    w_lat = jnp.zeros((w_in.shape[0], lat_dim), BF16).at[:, :w_in.shape[1]].set(w_in.astype(BF16))
    lat = _norm_mod_matmul(x, g, shift, scale, w_lat, F32)
    q, k, v = _mla_prep(lat, mla_tables, q_norm_g, kv_norm_g, w_uq, w_ukv, q_nope_g, q_rope_g, k_nope_g,
                        k_rope_g, bsz)
    o = _flash_attention(q, k, v)
    return _matmul_residual(o.reshape(x.shape[0], -1), w_out.astype(BF16), x, gate)


def _ret_layer(x, g, shift, scale, gate, bsz, ret_tables, w_in, norm_g, w_out):
    heads = w_out.shape[0] // RET_V
    proj = _norm_mod_matmul(x, g, shift, scale, w_in.astype(BF16), BF16, tn=1024)
    y = _retention(proj, ret_tables[0], ret_tables[1], norm_g, bsz, heads)
    return _matmul_residual(y, w_out.astype(BF16), x, gate)


def kernel(x, c, positions, ada_w, ada_b, norm1_g, norm2_g, ssd_w_in, ssd_conv_w, ssd_conv_b, ssd_dt_bias, ssd_a_log, ssd_d, ssd_norm_g, ssd_w_out, mla_w_in, mla_q_norm_g, mla_kv_norm_g, mla_w_uq, mla_w_ukv, mla_q_nope_g, mla_q_rope_g, mla_k_nope_g, mla_k_rope_g, mla_w_out, ret_w_in, ret_norm_g, ret_w_out, ffn_w_in, ffn_w_out, router_w, moe_w_in, moe_w_out):
    bsz, s, d = x.shape
    depth = ada_w.shape[0]
    mod = _ada_mod(c, ada_w, ada_b).reshape(depth, bsz, 6, 1, d)
    rc, rs, mc, ma, mb = _rope_tables(positions)
    xt = x.reshape(bsz * s, d)
    moe_wi = moe_w_in.astype(BF16).reshape((-1,) + moe_w_in.shape[2:])
    moe_wo = moe_w_out.astype(BF16).reshape((-1,) + moe_w_out.shape[2:])
    for i in range(depth):
        shift1, scale1, gate1, shift2, scale2, gate2 = (mod[i, :, m] for m in range(6))
        kind, j = i % N_MIXERS, i // N_MIXERS
        if kind == 0:
            xt = _ssd_layer(xt, norm1_g[i], shift1, scale1, gate1, bsz, ssd_w_in[j], ssd_conv_w[j], ssd_conv_b[j],
                            ssd_dt_bias[j], ssd_a_log[j], ssd_d[j], ssd_norm_g[j], ssd_w_out[j])
        elif kind == 1:
            xt = _mla_layer(xt, norm1_g[i], shift1, scale1, gate1, bsz, (mc, ma, mb), mla_w_in[j],
                            mla_q_norm_g[j], mla_kv_norm_g[j], mla_w_uq[j], mla_w_ukv[j], mla_q_nope_g[j],
                            mla_q_rope_g[j], mla_k_nope_g[j], mla_k_rope_g[j], mla_w_out[j])
        else:
            xt = _ret_layer(xt, norm1_g[i], shift1, scale1, gate1, bsz, (rc, rs), ret_w_in[j], ret_norm_g[j],
                            ret_w_out[j])
        if i % 2 == 0:
            xt = _ffn(xt, norm2_g[i], shift2, scale2, gate2, ffn_w_in[i // 2], ffn_w_out[i // 2])
        else:
            xt = _moe(xt, norm2_g[i], shift2, scale2, gate2, router_w[i // 2], moe_wi, moe_wo, i // 2)
    return xt.reshape(bsz, s, d)
```
